```python
import math
import jax
import jax.numpy as jnp
from jax import lax
import numpy as np

D_MODEL = 2048
BATCH = 1
SEQ = 8192
DEPTH = 2

F32 = jnp.float32
RET_HEADS = 4
RET_QK_DIM = 128
RET_V_DIM = 256
RET_CHUNK = 128
ROPE_BASE = 10000.0
GDN_HEADS = 8
GDN_DIM = 128
GDN_CHUNK = 64
CONV_WIDTH = 4
S5_CHANNELS = 1024
S5_GROUP = 16
S5_GROUPS = S5_CHANNELS // S5_GROUP
S5_STATE = 64
S5_STEP_MIN = 1e-3
S5_STEP_MAX = 1e-1
SSD_HEADS = 16
SSD_HEAD_DIM = 64
SSD_INNER = SSD_HEADS * SSD_HEAD_DIM
SSD_GROUPS = 2
SSD_STATE = 128
SSD_CHUNK = 128
DT_MIN = 1e-3
DT_MAX = 1e-1
D_FF = 4 * D_MODEL
N_BRANCH = 4
DEEPNORM_ALPHA = (2 * DEPTH) ** 0.25
DEEPNORM_BETA = (8 * DEPTH) ** -0.25
LN_EPS = 1e-5
RMS_EPS = 1e-6

RET_Q = RET_HEADS * RET_QK_DIM
RET_V = RET_HEADS * RET_V_DIM
GDN_W = GDN_HEADS * GDN_DIM
SSD_BC = SSD_GROUPS * SSD_STATE
SPLIT_SIZES = (
    RET_Q, RET_Q, RET_V, RET_V,
    GDN_W, GDN_W, GDN_W, GDN_W, GDN_HEADS, GDN_HEADS,
    S5_CHANNELS,
    SSD_INNER, SSD_INNER, SSD_BC, SSD_BC, SSD_HEADS,
    N_BRANCH * D_MODEL,
)
D_IN_PROJ = sum(SPLIT_SIZES)
SPLIT_POINTS = tuple(sum(SPLIT_SIZES[:i + 1]) for i in range(len(SPLIT_SIZES) - 1))

kernel_name = 'hybrid_gated_ret_gdn_s5_ssd_block'


def layer_norm(x, g, b):
    xf = x.astype(F32)
    mu = jnp.mean(xf, -1, keepdims=True)
    var = jnp.mean(jnp.square(xf - mu), -1, keepdims=True)
    return ((xf - mu) * lax.rsqrt(var + LN_EPS) * g.astype(F32) + b.astype(F32)).astype(x.dtype)


def rms_norm(x, w):
    xf = x.astype(F32)
    return xf * lax.rsqrt(jnp.mean(xf * xf, -1, keepdims=True) + RMS_EPS) * w.astype(F32)


def l2_norm(x):
    return x * lax.rsqrt(jnp.sum(x * x, -1, keepdims=True) + RMS_EPS)


def causal_conv(x, w):
    rhs = jnp.transpose(w).astype(x.dtype)[:, None, :]
    return lax.conv_general_dilated(x, rhs, window_strides=(1,), padding=[(CONV_WIDTH - 1, 0)],
                                    dimension_numbers=('NWC', 'WIO', 'NWC'),
                                    feature_group_count=x.shape[-1])


def rotary(x, positions):
    half = x.shape[-1] // 2
    inv = ROPE_BASE ** (-jnp.arange(half, dtype=F32) / half)
    ang = positions.astype(F32)[..., None] * inv
    cos = jnp.cos(ang)[:, :, None, :]
    sin = jnp.sin(ang)[:, :, None, :]
    x1, x2 = x[..., :half], x[..., half:]
    return jnp.concatenate([x1 * cos - x2 * sin, x1 * sin + x2 * cos], -1)


def chunk_decay_matrix(cum):
    c = cum.shape[-1]
    idx = jnp.arange(c)
    lower = idx[:, None] >= idx[None, :]
    diff = cum[..., :, None] - cum[..., None, :]
    return jnp.where(lower, jnp.exp(jnp.where(lower, diff, 0.0)), 0.0)


def retention(q, k, v, positions):
    bsz, t, h, dk = q.shape
    dv = v.shape[-1]
    c = RET_CHUNK
    n = t // c
    q = rotary(q.astype(F32), positions)
    k = rotary(k.astype(F32), positions) * dk ** -0.5
    v = v.astype(F32)
    log_g = jnp.log(1.0 - 2.0 ** (-5.0 - jnp.arange(h, dtype=F32)))
    idx = jnp.arange(c, dtype=F32)
    dmat = chunk_decay_matrix(idx[None, :] * log_g[:, None])
    q = q.reshape(bsz, n, c, h, dk)
    k = k.reshape(bsz, n, c, h, dk)
    v = v.reshape(bsz, n, c, h, dv)
    scores = jnp.einsum('bnihd,bnjhd->bnhij', q, k) * dmat
    inner = jnp.einsum('bnhij,bnjhe->bnihe', scores, v)
    k_decay = jnp.exp((c - 1.0 - idx)[:, None] * log_g[None, :])
    chunk_kv = jnp.einsum('bnjhd,jh,bnjhe->bnhde', k, k_decay, v)
    gamma_c = jnp.exp(c * log_g)[None, :, None, None]

    def step(r, kv):
        return r * gamma_c + kv, r

    _, r_prev = lax.scan(step, jnp.zeros((bsz, h, dk, dv), F32), jnp.moveaxis(chunk_kv, 1, 0))
    r_prev = jnp.moveaxis(r_prev, 0, 1)
    q_decay = jnp.exp((idx + 1.0)[:, None] * log_g[None, :])
    cross = jnp.einsum('bnihd,ih,bnhde->bnihe', q, q_decay, r_prev)
    return (inner + cross).reshape(bsz, t, h, dv)


def retention_mixer(rq, rk, rv, rg, positions):
    bsz, t, _ = rq.shape
    y = retention(rq.reshape(bsz, t, RET_HEADS, RET_QK_DIM), rk.reshape(bsz, t, RET_HEADS, RET_QK_DIM),
                  rv.reshape(bsz, t, RET_HEADS, RET_V_DIM), positions)
    mu = jnp.mean(y, -1, keepdims=True)
    var = jnp.mean(jnp.square(y - mu), -1, keepdims=True)
    y = ((y - mu) * lax.rsqrt(var + LN_EPS)).reshape(bsz, t, RET_V)
    return jax.nn.silu(rg.astype(F32)) * y


def gated_delta_rule(q, k, v, g, beta):
    bsz, t, h, dk = q.shape
    dv = v.shape[-1]
    c = GDN_CHUNK
    n = t // c
    q = (q * dk ** -0.5).reshape(bsz, n, c, h, dk)
    k = k.reshape(bsz, n, c, h, dk)
    v = v.reshape(bsz, n, c, h, dv)
    beta = beta.reshape(bsz, n, c, h)
    gcum = jnp.cumsum(g.reshape(bsz, n, c, h), axis=2)
    gamma = chunk_decay_matrix(jnp.swapaxes(gcum, 2, 3))
    idx = jnp.arange(c)
    strict = idx[:, None] > idx[None, :]
    kk = jnp.einsum('bnihd,bnjhd->bnhij', k, k)
    m = jnp.where(strict, jnp.swapaxes(beta, 2, 3)[..., :, None] * kk * gamma, 0.0)
    a_mat = m + jnp.eye(c, dtype=F32)
    rhs = jnp.concatenate([v * beta[..., None], k * (beta * jnp.exp(gcum))[..., None]], -1)
    rhs = jnp.swapaxes(rhs, 2, 3)
    sol = lax.linalg.triangular_solve(a_mat, rhs, left_side=True, lower=True, unit_diagonal=True)
    u = sol[..., :dv]
    w = sol[..., dv:]
    qk = jnp.einsum('bnihd,bnjhd->bnhij', q, k) * gamma
    q_dec = jnp.swapaxes(q * jnp.exp(gcum)[..., None], 2, 3)
    k_tail = jnp.swapaxes(k * jnp.exp(gcum[:, :, -1:, :] - gcum)[..., None], 2, 3)
    g_last = jnp.exp(gcum[:, :, -1, :])

    def step(s, xs):
        u_c, w_c, qk_c, qd_c, kt_c, gl_c = xs
        v_new = u_c - jnp.einsum('bhcd,bhde->bhce', w_c, s)
        o = jnp.einsum('bhcd,bhde->bhce', qd_c, s) + jnp.einsum('bhij,bhje->bhie', qk_c, v_new)
        s = s * gl_c[..., None, None] + jnp.einsum('bhcd,bhce->bhde', kt_c, v_new)
        return s, o

    xs = tuple(jnp.moveaxis(a, 1, 0) for a in (u, w, qk, q_dec, k_tail, g_last))
    _, o = lax.scan(step, jnp.zeros((bsz, h, dk, dv), F32), xs)
    o = jnp.swapaxes(jnp.moveaxis(o, 0, 1), 2, 3)
    return o.reshape(bsz, t, h, dv)


def gated_deltanet_mixer(dq, dk_, dv_, dz, da, db, conv_w, a_log, dt_bias, norm_w):
    bsz, t, _ = dq.shape
    qkv = jax.nn.silu(causal_conv(jnp.concatenate([dq, dk_, dv_], -1), conv_w).astype(F32))
    q, k, v = jnp.split(qkv, 3, axis=-1)
    heads = lambda a: a.reshape(bsz, t, GDN_HEADS, GDN_DIM)
    q = l2_norm(heads(q))
    k = l2_norm(heads(k))
    g = -jnp.exp(a_log.astype(F32)) * jax.nn.softplus(da.astype(F32) + dt_bias.astype(F32))
    beta = jax.nn.sigmoid(db.astype(F32))
    o = gated_delta_rule(q, k, heads(v), g, beta)
    o = rms_norm(o, norm_w) * jax.nn.silu(heads(dz).astype(F32))
    return o.reshape(bsz, t, GDN_W)


def s5_mixer(u, lam_re, lam_im, log_step, b_re, b_im, c_re, c_im, d_skip, glu_w, glu_b):
    bsz, t, _ = u.shape
    uf = u.astype(F32)
    ug = uf.reshape(bsz, t, S5_GROUPS, S5_GROUP)
    lre = lam_re.astype(F32)
    lim = lam_im.astype(F32)
    step = jnp.exp(log_step.astype(F32))[:, None]
    mag = jnp.exp(lre * step)
    ab_re = mag * jnp.cos(lim * step)
    ab_im = mag * jnp.sin(lim * step)
    den = lre * lre + lim * lim
    f_re = ((ab_re - 1.0) * lre + ab_im * lim) / den
    f_im = (ab_im * lre - (ab_re - 1.0) * lim) / den
    bb_re = f_re[..., None] * b_re.astype(F32) - f_im[..., None] * b_im.astype(F32)
    bb_im = f_re[..., None] * b_im.astype(F32) + f_im[..., None] * b_re.astype(F32)
    bu_re = jnp.einsum('btgc,gnc->btgn', ug, bb_re)
    bu_im = jnp.einsum('btgc,gnc->btgn', ug, bb_im)
    a_re = jnp.broadcast_to(ab_re, bu_re.shape)
    a_im = jnp.broadcast_to(ab_im, bu_im.shape)

    def combine(e1, e2):
        a1r, a1i, b1r, b1i = e1
        a2r, a2i, b2r, b2i = e2
        return (a2r * a1r - a2i * a1i, a2r * a1i + a2i * a1r,
                a2r * b1r - a2i * b1i + b2r, a2r * b1i + a2i * b1r + b2i)

    _, _, xr, xi = lax.associative_scan(combine, (a_re, a_im, bu_re, bu_im), axis=1)
    y = (jnp.einsum('gcn,btgn->btgc', c_re.astype(F32), xr)
         - jnp.einsum('gcn,btgn->btgc', c_im.astype(F32), xi))
    y = y.reshape(bsz, t, S5_CHANNELS) + d_skip.astype(F32) * uf
    y = jax.nn.gelu(y)
    return y * jax.nn.sigmoid(y @ glu_w.astype(F32) + glu_b.astype(F32))


def ssd_scan(x, dt, a, bm, cm):
    bsz, t, h, p = x.shape
    nst = bm.shape[-1]
    c = SSD_CHUNK
    n = t // c
    xdt = (x * dt[..., None]).reshape(bsz, n, c, h, p)
    acum = jnp.cumsum((dt * a).reshape(bsz, n, c, h), axis=2)
    bm = bm.reshape(bsz, n, c, h, nst)
    cm = cm.reshape(bsz, n, c, h, nst)
    lmat = chunk_decay_matrix(jnp.swapaxes(acum, 2, 3))
    scores = jnp.einsum('bnihs,bnjhs->bnhij', cm, bm) * lmat
    y_diag = jnp.einsum('bnhij,bnjhp->bnihp', scores, xdt)
    decay_states = jnp.exp(acum[:, :, -1:, :] - acum)
    states = jnp.einsum('bnjhs,bnjh,bnjhp->bnhps', bm, decay_states, xdt)
    chunk_decay = jnp.exp(acum[:, :, -1, :])

    def step(s, xs):
        st, dec = xs
        return s * dec[..., None, None] + st, s

    _, prev = lax.scan(step, jnp.zeros((bsz, h, p, nst), F32),
                       (jnp.moveaxis(states, 1, 0), jnp.moveaxis(chunk_decay, 1, 0)))
    prev = jnp.moveaxis(prev, 0, 1)
    y_off = jnp.einsum('bnihs,bnhps,bnih->bnihp', cm, prev, jnp.exp(acum))
    return (y_diag + y_off).reshape(bsz, t, h, p)


def mamba2_mixer(mz, mx, mb, mc, mdt, conv_w, conv_b, dt_bias, a_log, d_skip, norm_w):
    bsz, t, _ = mx.shape
    xbc = causal_conv(jnp.concatenate([mx, mb, mc], -1), conv_w) + conv_b
    xbc = jax.nn.silu(xbc.astype(F32))
    xs, bs, cs = jnp.split(xbc, [SSD_INNER, SSD_INNER + SSD_BC], axis=-1)
    xh = xs.reshape(bsz, t, SSD_HEADS, SSD_HEAD_DIM)
    rep = SSD_HEADS // SSD_GROUPS
    bh = jnp.repeat(bs.reshape(bsz, t, SSD_GROUPS, SSD_STATE), rep, axis=2)
    ch = jnp.repeat(cs.reshape(bsz, t, SSD_GROUPS, SSD_STATE), rep, axis=2)
    dt = jax.nn.softplus(mdt.astype(F32) + dt_bias.astype(F32))
    a = -jnp.exp(a_log.astype(F32))
    y = ssd_scan(xh, dt, a, bh, ch) + d_skip.astype(F32)[:, None] * xh
    y = y.reshape(bsz, t, SSD_INNER) * jax.nn.silu(mz.astype(F32))
    y = rms_norm(y.reshape(bsz, t, SSD_GROUPS, SSD_INNER // SSD_GROUPS),
                 norm_w.reshape(SSD_GROUPS, SSD_INNER // SSD_GROUPS))
    return y.reshape(bsz, t, SSD_INNER)


def setup_inputs(seed: int = 0) -> dict:
    key = jax.random.key(seed)
    keys = list(jax.random.split(key, 40))
    L = DEPTH

    def normal(shape, scale):
        return jax.random.normal(keys.pop(), shape, F32) * scale

    def uniform(shape, lo, hi):
        return jax.random.uniform(keys.pop(), shape, F32, lo, hi)

    def dt_bias(shape):
        dt = jnp.exp(uniform(shape, math.log(DT_MIN), math.log(DT_MAX)))
        return dt + jnp.log(-jnp.expm1(-dt))

    gain = lambda shape: 1.0 + normal(shape, 0.02)
    n_idx = jnp.arange(S5_STATE, dtype=F32)
    return {
        'x': normal((BATCH, SEQ, D_MODEL), 1.0),
        'positions': jnp.broadcast_to(jnp.arange(SEQ, dtype=jnp.int32)[None, :], (BATCH, SEQ)),
        'w_in': normal((L, D_MODEL, D_IN_PROJ), D_MODEL ** -0.5),
        'gdn_conv_w': normal((L, 3 * GDN_W, CONV_WIDTH), CONV_WIDTH ** -0.5),
        'gdn_a_log': jnp.log(uniform((L, GDN_HEADS), 1.0, 16.0)),
        'gdn_dt_bias': dt_bias((L, GDN_HEADS)),
        'gdn_norm_w': gain((L, GDN_DIM)),
        's5_lam_re': -0.5 * (1.0 + normal((L, S5_GROUPS, S5_STATE), 0.01)),
        's5_lam_im': jnp.broadcast_to(math.pi * n_idx, (L, S5_GROUPS, S5_STATE)),
        's5_log_step': uniform((L, S5_GROUPS), math.log(S5_STEP_MIN), math.log(S5_STEP_MAX)),
        's5_b_re': normal((L, S5_GROUPS, S5_STATE, S5_GROUP), (2 * S5_GROUP) ** -0.5),
        's5_b_im': normal((L, S5_GROUPS, S5_STATE, S5_GROUP), (2 * S5_GROUP) ** -0.5),
        's5_c_re': normal((L, S5_GROUPS, S5_GROUP, S5_STATE), S5_STATE ** -0.5),
        's5_c_im': normal((L, S5_GROUPS, S5_GROUP, S5_STATE), S5_STATE ** -0.5),
        's5_d': normal((L, S5_CHANNELS), 1.0),
        's5_glu_w': normal((L, S5_CHANNELS, S5_CHANNELS), S5_CHANNELS ** -0.5),
        's5_glu_b': normal((L, S5_CHANNELS), 0.01),
        'ssd_conv_w': normal((L, SSD_INNER + 2 * SSD_BC, CONV_WIDTH), CONV_WIDTH ** -0.5),
        'ssd_conv_b': normal((L, SSD_INNER + 2 * SSD_BC), 0.01),
        'ssd_dt_bias': dt_bias((L, SSD_HEADS)),
        'ssd_a_log': jnp.log(uniform((L, SSD_HEADS), 1.0, 16.0)),
        'ssd_d': gain((L, SSD_HEADS)),
        'ssd_norm_w': gain((L, SSD_INNER)),
        'w_branch_ret': normal((L, RET_V, D_MODEL), DEEPNORM_BETA * RET_V ** -0.5),
        'w_branch_gdn': normal((L, GDN_W, D_MODEL), DEEPNORM_BETA * GDN_W ** -0.5),
        'w_branch_s5': normal((L, S5_CHANNELS, D_MODEL), DEEPNORM_BETA * S5_CHANNELS ** -0.5),
        'w_branch_ssd': normal((L, SSD_INNER, D_MODEL), DEEPNORM_BETA * SSD_INNER ** -0.5),
        'w_out': normal((L, D_MODEL, D_MODEL), DEEPNORM_BETA * D_MODEL ** -0.5),
        'ln1_g': gain((L, D_MODEL)),
        'ln1_b': normal((L, D_MODEL), 0.02),
        'w_up': normal((L, D_MODEL, D_FF), DEEPNORM_BETA * D_MODEL ** -0.5),
        'w_down': normal((L, D_FF, D_MODEL), DEEPNORM_BETA * D_FF ** -0.5),
        'ln2_g': gain((L, D_MODEL)),
        'ln2_b': normal((L, D_MODEL), 0.02),
    }


def reference(x, positions, w_in, gdn_conv_w, gdn_a_log, gdn_dt_bias, gdn_norm_w,
              s5_lam_re, s5_lam_im, s5_log_step, s5_b_re, s5_b_im, s5_c_re, s5_c_im,
              s5_d, s5_glu_w, s5_glu_b, ssd_conv_w, ssd_conv_b, ssd_dt_bias, ssd_a_log,
              ssd_d, ssd_norm_w, w_branch_ret, w_branch_gdn, w_branch_s5, w_branch_ssd,
              w_out, ln1_g, ln1_b, w_up, w_down, ln2_g, ln2_b):
    bsz, t, _ = x.shape
    dt_ = x.dtype
    for l in range(DEPTH):
        proj = x @ w_in[l]
        (r_q, r_k, r_v, r_g, d_q, d_k, d_v, d_z, d_a, d_b, s_u,
         m_z, m_x, m_b, m_c, m_dt, gate_logits) = jnp.split(proj, SPLIT_POINTS, axis=-1)
        y_ret = retention_mixer(r_q, r_k, r_v, r_g, positions)
        y_gdn = gated_deltanet_mixer(d_q, d_k, d_v, d_z, d_a, d_b, gdn_conv_w[l], gdn_a_log[l],
                                     gdn_dt_bias[l], gdn_norm_w[l])
        y_s5 = s5_mixer(s_u, s5_lam_re[l], s5_lam_im[l], s5_log_step[l], s5_b_re[l], s5_b_im[l],
                        s5_c_re[l], s5_c_im[l], s5_d[l], s5_glu_w[l], s5_glu_b[l])
        y_ssd = mamba2_mixer(m_z, m_x, m_b, m_c, m_dt, ssd_conv_w[l], ssd_conv_b[l], ssd_dt_bias[l],
                             ssd_a_log[l], ssd_d[l], ssd_norm_w[l])
        gates = jax.nn.sigmoid(gate_logits.reshape(bsz, t, N_BRANCH, D_MODEL))
        merged = (gates[:, :, 0] * (y_ret.astype(dt_) @ w_branch_ret[l])
                  + gates[:, :, 1] * (y_gdn.astype(dt_) @ w_branch_gdn[l])
                  + gates[:, :, 2] * (y_s5.astype(dt_) @ w_branch_s5[l])
                  + gates[:, :, 3] * (y_ssd.astype(dt_) @ w_branch_ssd[l]))
        mix = merged @ w_out[l]
        x = layer_norm(DEEPNORM_ALPHA * x + mix, ln1_g[l], ln1_b[l])
        hid = jnp.square(jax.nn.relu(x @ w_up[l])) @ w_down[l]
        x = layer_norm(DEEPNORM_ALPHA * x + hid, ln2_g[l], ln2_b[l])
    return x
```

```python
import functools
import math

import numpy as np
import jax
import jax.numpy as jnp
from jax import lax
from jax.experimental import pallas as pl
from jax.experimental.pallas import tpu as pltpu

F32 = jnp.float32
BF16 = jnp.bfloat16
HI = lax.Precision.HIGHEST

D_MODEL = 2048
DEPTH = 2
RET_HEADS, RET_QK, RET_V, RET_CHUNK = 4, 128, 256, 128
ROPE_BASE = 10000.0
GDN_HEADS, GDN_DIM, GDN_CHUNK = 8, 128, 64
CONV_WIDTH = 4
S5_CH, S5_GROUP, S5_STATE = 1024, 16, 64
S5_GROUPS = S5_CH // S5_GROUP
S5_BLK = 16
SSD_HEADS, SSD_HD, SSD_GROUPS, SSD_STATE, SSD_CHUNK = 16, 64, 2, 128, 128
SSD_INNER = SSD_HEADS * SSD_HD
D_FF = 4 * D_MODEL
ALPHA = (2 * DEPTH) ** 0.25
LN_EPS = 1e-5
RMS_EPS = 1e-6

C_RQ, C_RK, C_RV, C_RG = 0, 512, 1024, 2048
C_DQ, C_DK, C_DV, C_DZ = 3072, 4096, 5120, 6144
C_SU, C_MZ, C_MX, C_MB, C_MC = 7168, 8192, 9216, 10240, 10496
C_GATE = 10752
C_SMALL = 18944
NP = 19456

VMEM_LIMIT = 52 * 1024 * 1024


def _cp(*sem):
    return pltpu.CompilerParams(dimension_semantics=sem, vmem_limit_bytes=VMEM_LIMIT)


def _dot(a, b, prec=None):
    return lax.dot_general(a, b, (((1,), (0,)), ((), ())), precision=prec, preferred_element_type=F32)


def _dot_nt(a, b, prec=None):
    return lax.dot_general(a, b, (((1,), (1,)), ((), ())), precision=prec, preferred_element_type=F32)


def _dot_tn(a, b, prec=None):
    return lax.dot_general(a, b, (((0,), (0,)), ((), ())), precision=prec, preferred_element_type=F32)


def _bdot(a, b):
    return _dot(a.astype(BF16), b.astype(BF16))


def _bdot_nt(a, b):
    return _dot_nt(a.astype(BF16), b.astype(BF16))


def _bdot_tn(a, b):
    return _dot_tn(a.astype(BF16), b.astype(BF16))


def _silu(x):
    return x * (1.0 / (1.0 + jnp.exp(-x)))


def _sigmoid(x):
    return 1.0 / (1.0 + jnp.exp(-x))


def _softplus(x):
    return jnp.maximum(x, 0.0) + jnp.log1p(jnp.exp(-jnp.abs(x)))


def _masked_decay(diff, lower):
    return jnp.where(lower, jnp.exp(jnp.where(lower, diff, 0.0)), 0.0)


def _pair_diff(cum, lane, l0):
    a = jnp.where(lane == l0, cum, jnp.where(lane == l0 + 1, 1.0, 0.0))
    b = jnp.where(lane == l0, 1.0, jnp.where(lane == l0 + 1, -cum, 0.0))
    return _dot_nt(a, b, HI)


def _shifted(x, tail, s, row8):
    xs = pltpu.roll(x, s, 0)
    ts = pltpu.roll(tail, s, 0)
    first = jnp.where(row8 < s, ts, xs[:8])
    return jnp.concatenate([first, xs[8:]], axis=0)


def _causal_conv(x, tail, w):
    row8 = lax.broadcasted_iota(jnp.int32, (8, x.shape[1]), 0)
    acc = x * w[3:4, :]
    for s in (1, 2, 3):
        acc = acc + _shifted(x, tail, s, row8) * w[3 - s:4 - s, :]
    return acc


def _mm_kernel(a_ref, b_ref, o_ref, *, relu2):
    acc = jnp.dot(a_ref[...], b_ref[...], preferred_element_type=F32)
    if relu2:
        acc = jnp.square(jnp.maximum(acc, 0.0))
    o_ref[...] = acc.astype(o_ref.dtype)


def _matmul(a, b, out_dtype, tm, tn, relu2=False):
    m, k = a.shape
    n = b.shape[1]
    return pl.pallas_call(
        functools.partial(_mm_kernel, relu2=relu2),
        out_shape=jax.ShapeDtypeStruct((m, n), out_dtype),
        grid=(n // tn, m // tm),
        in_specs=[pl.BlockSpec((tm, k), lambda j, i: (i, 0)),
                  pl.BlockSpec((k, tn), lambda j, i: (0, j))],
        out_specs=pl.BlockSpec((tm, tn), lambda j, i: (i, j)),
        compiler_params=_cp("arbitrary", "arbitrary"),
        name="matmul",
    )(a, b)


def _mm_ln_kernel(a_ref, b_ref, res_ref, g_ref, beta_ref, of_ref, ob_ref, acc_ref, *, nk):
    k = pl.program_id(1)

    @pl.when(k == 0)
    def _():
        acc_ref[...] = jnp.zeros_like(acc_ref)

    acc_ref[...] += jnp.dot(a_ref[...], b_ref[...], preferred_element_type=F32)

    @pl.when(k == nk - 1)
    def _():
        y = ALPHA * res_ref[...] + acc_ref[...]
        mu = jnp.mean(y, axis=-1, keepdims=True)
        d = y - mu
        var = jnp.mean(d * d, axis=-1, keepdims=True)
        out = d * lax.rsqrt(var + LN_EPS) * g_ref[...] + beta_ref[...]
        of_ref[...] = out
        ob_ref[...] = out.astype(BF16)


def _matmul_res_ln(a, b, res, g, beta, tm, tk):
    m, k = a.shape
    n = b.shape[1]
    nk = k // tk
    return pl.pallas_call(
        functools.partial(_mm_ln_kernel, nk=nk),
        out_shape=(jax.ShapeDtypeStruct((m, n), F32), jax.ShapeDtypeStruct((m, n), BF16)),
        grid=(m // tm, nk),
        in_specs=[pl.BlockSpec((tm, tk), lambda i, kk: (i, kk)),
                  pl.BlockSpec((tk, n), lambda i, kk: (kk, 0)),
                  pl.BlockSpec((tm, n), lambda i, kk: (i, 0)),
                  pl.BlockSpec((1, n), lambda i, kk: (0, 0)),
                  pl.BlockSpec((1, n), lambda i, kk: (0, 0))],
        out_specs=(pl.BlockSpec((tm, n), lambda i, kk: (i, 0)),
                   pl.BlockSpec((tm, n), lambda i, kk: (i, 0))),
        scratch_shapes=[pltpu.VMEM((tm, n), F32)],
        compiler_params=_cp("arbitrary", "arbitrary"),
        name="matmul_res_ln",
    )(a, b, res, g.reshape(1, n), beta.reshape(1, n))


def _merge_kernel(y0, y1, y2, y3, w0, w1, w2, w3, g0, g1, g2, g3, o_ref):
    acc = None
    for y_ref, w_ref, g_ref in ((y0, w0, g0), (y1, w1, g1), (y2, w2, g2), (y3, w3, g3)):
        t = _sigmoid(g_ref[...]) * jnp.dot(y_ref[...], w_ref[...], preferred_element_type=F32)
        acc = t if acc is None else acc + t
    o_ref[...] = acc.astype(o_ref.dtype)


def _merge(p, ys, ws, tm, tn):
    t = p.shape[0]
    kb = ys[0].shape[1]
    gate_blk = C_GATE // tn
    per_branch = D_MODEL // tn
    y_specs = [pl.BlockSpec((tm, kb), lambda i, j: (i, 0)) for _ in range(4)]
    w_specs = [pl.BlockSpec((kb, tn), lambda i, j: (0, j)) for _ in range(4)]
    g_specs = [pl.BlockSpec((tm, tn), functools.partial(lambda i, j, b: (i, gate_blk + b * per_branch + j), b=b))
               for b in range(4)]
    return pl.pallas_call(
        _merge_kernel,
        out_shape=jax.ShapeDtypeStruct((t, D_MODEL), BF16),
        grid=(t // tm, D_MODEL // tn),
        in_specs=y_specs + w_specs + g_specs,
        out_specs=pl.BlockSpec((tm, tn), lambda i, j: (i, j)),
        compiler_params=_cp("arbitrary", "arbitrary"),
        name="merge",
    )(*ys, *ws, p, p, p, p)


def _glu_kernel(v_ref, w_ref, b_ref, o_ref):
    v = v_ref[...]
    z = jnp.dot(v.astype(BF16), w_ref[...], preferred_element_type=F32) + b_ref[...]
    o_ref[...] = (v * _sigmoid(z)).astype(o_ref.dtype)


def _glu(v, w, b, tm):
    t, n = v.shape
    return pl.pallas_call(
        _glu_kernel,
        out_shape=jax.ShapeDtypeStruct((t, n), BF16),
        grid=(t // tm,),
        in_specs=[pl.BlockSpec((tm, n), lambda i: (i, 0)),
                  pl.BlockSpec((n, n), lambda i: (0, 0)),
                  pl.BlockSpec((1, n), lambda i: (0, 0))],
        out_specs=pl.BlockSpec((tm, n), lambda i: (i, 0)),
        compiler_params=_cp("arbitrary"),
        name="s5_glu",
    )(v, w, b.reshape(1, n))


def _rope_kernel(pos_ref, inv_ref, cos_ref, sin_ref):
    ang = pos_ref[...].astype(F32) * inv_ref[...]
    lane = lax.broadcasted_iota(jnp.int32, ang.shape, 1)
    s = jnp.sin(ang)
    cos_ref[...] = jnp.cos(ang)
    sin_ref[...] = jnp.where(lane < RET_QK // 2, -s, s)


def _rope_tables(positions, tb):
    t = positions.shape[0]
    half = RET_QK // 2
    inv = (ROPE_BASE ** (-np.arange(half, dtype=np.float64) / half)).astype(np.float32)
    inv = jnp.asarray(np.concatenate([inv, inv])[None, :])
    return pl.pallas_call(
        _rope_kernel,
        out_shape=(jax.ShapeDtypeStruct((t, RET_QK), F32), jax.ShapeDtypeStruct((t, RET_QK), F32)),
        grid=(t // tb,),
        in_specs=[pl.BlockSpec((tb, 1), lambda i: (i, 0)),
                  pl.BlockSpec((1, RET_QK), lambda i: (0, 0))],
        out_specs=(pl.BlockSpec((tb, RET_QK), lambda i: (i, 0)),
                   pl.BlockSpec((tb, RET_QK), lambda i: (i, 0))),
        compiler_params=_cp("arbitrary"),
        name="rope_tables",
    )(positions.reshape(t, 1), inv)


def _ret_kernel(q_ref, k_ref, v_ref, g_ref, cos_ref, sin_ref, dmat_ref, qdec_ref, kdec_ref, gam_ref,
                o_ref, r_ref, *, nchunk):
    @pl.when(pl.program_id(1) == 0)
    def _():
        r_ref[...] = jnp.zeros_like(r_ref)

    c = RET_CHUNK
    dmat = dmat_ref[...]
    qdec = qdec_ref[...]
    kdec = kdec_ref[...]
    gam = gam_ref[0:1, :]
    for ci in range(nchunk):
        rows = slice(ci * c, (ci + 1) * c)
        cos = cos_ref[rows, :]
        sin = sin_ref[rows, :]
        q = q_ref[rows, :]
        k = k_ref[rows, :]
        v = v_ref[rows, :]
        qr = q * cos + pltpu.roll(q, RET_QK // 2, 1) * sin
        kr = (k * cos + pltpu.roll(k, RET_QK // 2, 1) * sin) * (RET_QK ** -0.5)
        r = r_ref[...]
        scores = _bdot_nt(qr, kr) * dmat
        y = _bdot(scores, v) + _bdot(qr * qdec, r)
        r_ref[...] = r * gam + _bdot_tn(kr * kdec, v)
        mu = jnp.mean(y, axis=-1, keepdims=True)
        d = y - mu
        var = jnp.mean(d * d, axis=-1, keepdims=True)
        yn = d * lax.rsqrt(var + LN_EPS)
        o_ref[rows, :] = (_silu(g_ref[rows, :]) * yn).astype(o_ref.dtype)


def _ret_constants():
    c = RET_CHUNK
    h = np.arange(RET_HEADS, dtype=np.float64)
    log_g = np.log(1.0 - 2.0 ** (-5.0 - h))
    idx = np.arange(c, dtype=np.float64)
    diff = idx[:, None] - idx[None, :]
    dmat = np.where(diff >= 0, np.exp(np.where(diff >= 0, diff, 0.0)[None] * log_g[:, None, None]), 0.0)
    qdec = np.exp((idx + 1.0)[None, :] * log_g[:, None])
    kdec = np.exp((c - 1.0 - idx)[None, :] * log_g[:, None])
    gam = np.exp(c * log_g)
    qdec = np.broadcast_to(qdec[:, :, None], (RET_HEADS, c, RET_QK))
    kdec = np.broadcast_to(kdec[:, :, None], (RET_HEADS, c, RET_QK))
    gam = np.broadcast_to(gam[:, None, None], (RET_HEADS, 8, RET_V))
    f = lambda a: jnp.asarray(np.ascontiguousarray(a), dtype=F32)
    return f(dmat), f(qdec), f(kdec), f(gam)


def _retention(p, cos, sin, tb):
    t = p.shape[0]
    dmat, qdec, kdec, gam = _ret_constants()
    c = RET_CHUNK
    return pl.pallas_call(
        functools.partial(_ret_kernel, nchunk=tb // c),
        out_shape=jax.ShapeDtypeStruct((t, RET_HEADS * RET_V), BF16),
        grid=(RET_HEADS, t // tb),
        in_specs=[pl.BlockSpec((tb, RET_QK), lambda h, i: (i, C_RQ // RET_QK + h)),
                  pl.BlockSpec((tb, RET_QK), lambda h, i: (i, C_RK // RET_QK + h)),
                  pl.BlockSpec((tb, RET_V), lambda h, i: (i, C_RV // RET_V + h)),
                  pl.BlockSpec((tb, RET_V), lambda h, i: (i, C_RG // RET_V + h)),
                  pl.BlockSpec((tb, RET_QK), lambda h, i: (i, 0)),
                  pl.BlockSpec((tb, RET_QK), lambda h, i: (i, 0)),
                  pl.BlockSpec((None, c, c), lambda h, i: (h, 0, 0)),
                  pl.BlockSpec((None, c, RET_QK), lambda h, i: (h, 0, 0)),
                  pl.BlockSpec((None, c, RET_QK), lambda h, i: (h, 0, 0)),
                  pl.BlockSpec((None, 8, RET_V), lambda h, i: (h, 0, 0))],
        out_specs=pl.BlockSpec((tb, RET_V), lambda h, i: (i, h)),
        scratch_shapes=[pltpu.VMEM((RET_QK, RET_V), F32)],
        compiler_params=_cp("arbitrary", "arbitrary"),
        name="retention",
    )(p, p, p, p, cos, sin, dmat, qdec, kdec, gam)


def _gdn_kernel(q_ref, k_ref, v_ref, z_ref, sm_ref, cw_ref, hp_ref, nw_ref, o_ref,
                s_ref, tq_ref, tk_ref, tv_ref, *, nchunk):
    h = pl.program_id(0)

    @pl.when(pl.program_id(1) == 0)
    def _():
        s_ref[...] = jnp.zeros_like(s_ref)
        tq_ref[...] = jnp.zeros_like(tq_ref)
        tk_ref[...] = jnp.zeros_like(tk_ref)
        tv_ref[...] = jnp.zeros_like(tv_ref)

    c = GDN_CHUNK
    tb = nchunk * c
    xq, xk, xv = q_ref[...], k_ref[...], v_ref[...]
    q = _silu(_causal_conv(xq, tq_ref[...], cw_ref[0]))
    k = _silu(_causal_conv(xk, tk_ref[...], cw_ref[1]))
    v = _silu(_causal_conv(xv, tv_ref[...], cw_ref[2]))
    tq_ref[...] = xq[tb - 8:, :]
    tk_ref[...] = xk[tb - 8:, :]
    tv_ref[...] = xv[tb - 8:, :]
    q = q * lax.rsqrt(jnp.sum(q * q, axis=-1, keepdims=True) + RMS_EPS) * (GDN_DIM ** -0.5)
    k = k * lax.rsqrt(jnp.sum(k * k, axis=-1, keepdims=True) + RMS_EPS)

    sm = sm_ref[...]
    lane_b = lax.broadcasted_iota(jnp.int32, sm.shape, 1)
    da = jnp.sum(jnp.where(lane_b == h, sm, 0.0), axis=-1, keepdims=True)
    db = jnp.sum(jnp.where(lane_b == h + GDN_HEADS, sm, 0.0), axis=-1, keepdims=True)
    a_log = hp_ref[0:1, 0:1]
    dt_bias = hp_ref[1:2, 0:1]
    g_all = -jnp.exp(a_log) * _softplus(da + dt_bias)
    beta_all = _sigmoid(db)

    ii = lax.broadcasted_iota(jnp.int32, (c, c), 0)
    jj = lax.broadcasted_iota(jnp.int32, (c, c), 1)
    lower = ii >= jj
    strict = ii > jj
    ltri = lower.astype(F32)
    eye = (ii == jj).astype(F32)
    lane = lax.broadcasted_iota(jnp.int32, (c, GDN_DIM), 1)
    nw = nw_ref[...]

    for ci in range(nchunk):
        rows = slice(ci * c, (ci + 1) * c)
        qc, kc, vc = q[rows], k[rows], v[rows]
        bc = beta_all[rows]
        gcum = _dot(ltri, jnp.broadcast_to(g_all[rows], (c, GDN_DIM)), HI)
        gamma = _masked_decay(_pair_diff(gcum, lane, 0), lower)
        kk = _dot_nt(kc, kc, HI)
        nmat = jnp.where(strict, -(bc * kk * gamma), 0.0)
        inv = eye + nmat
        pw = nmat
        for _ in range(5):
            pw = _dot(pw, pw, HI)
            inv = inv + _dot(inv, pw, HI)
        eg = jnp.exp(gcum)
        u = _dot(inv, vc * bc, HI)
        w = _dot(inv, kc * (bc * eg), HI)
        qk = _bdot_nt(qc, kc) * gamma
        glast = gcum[c - 1:c, :]
        kt = kc * jnp.exp(glast - gcum)
        s = s_ref[...]
        vnew = u - _bdot(w, s)
        o = _bdot(qc * eg, s) + _bdot(qk, vnew)
        s_ref[...] = s * jnp.exp(glast) + _bdot_tn(kt, vnew)
        o = o * lax.rsqrt(jnp.mean(o * o, axis=-1, keepdims=True) + RMS_EPS) * nw
        o_ref[rows, :] = (o * _silu(z_ref[rows, :])).astype(o_ref.dtype)


def _gdn(p, conv_w, a_log, dt_bias, norm_w, tb):
    t = p.shape[0]
    hd = GDN_DIM
    cw = conv_w.reshape(3, GDN_HEADS, hd, CONV_WIDTH).transpose(0, 1, 3, 2)
    cw = jnp.pad(cw, ((0, 0), (0, 0), (0, 8 - CONV_WIDTH), (0, 0)))
    hp = jnp.stack([a_log, dt_bias], axis=1)
    hp = jnp.broadcast_to(jnp.pad(hp, ((0, 0), (0, 6)))[:, :, None], (GDN_HEADS, 8, hd))
    blk = lambda off: (lambda h, i: (i, off // hd + h))
    return pl.pallas_call(
        functools.partial(_gdn_kernel, nchunk=tb // GDN_CHUNK),
        out_shape=jax.ShapeDtypeStruct((t, GDN_HEADS * hd), BF16),
        grid=(GDN_HEADS, t // tb),
        in_specs=[pl.BlockSpec((tb, hd), blk(C_DQ)),
                  pl.BlockSpec((tb, hd), blk(C_DK)),
                  pl.BlockSpec((tb, hd), blk(C_DV)),
                  pl.BlockSpec((tb, hd), blk(C_DZ)),
                  pl.BlockSpec((tb, 128), lambda h, i: (i, C_SMALL // 128)),
                  pl.BlockSpec((3, None, 8, hd), lambda h, i: (0, h, 0, 0)),
                  pl.BlockSpec((None, 8, hd), lambda h, i: (h, 0, 0)),
                  pl.BlockSpec((1, hd), lambda h, i: (0, 0))],
        out_specs=pl.BlockSpec((tb, hd), lambda h, i: (i, h)),
        scratch_shapes=[pltpu.VMEM((hd, hd), F32), pltpu.VMEM((8, hd), F32),
                        pltpu.VMEM((8, hd), F32), pltpu.VMEM((8, hd), F32)],
        compiler_params=_cp("arbitrary", "arbitrary"),
        name="gated_deltanet",
    )(p, p, p, p, p, cw, hp, norm_w.reshape(1, hd))


def _gelu_tanh(x):
    return 0.5 * x * (1.0 + jnp.tanh(math.sqrt(2.0 / math.pi) * (x + 0.044715 * (x * x * x))))


def _s5_kernel(u_ref, t0_ref, bm_ref, cm_ref, apr_ref, api_ref, dsk_ref, o_ref, *, nsteps):
    u = u_ref[...]
    y = _dot(u, t0_ref[...], HI)
    x = _dot(u, bm_ref[...], HI)
    row = lax.broadcasted_iota(jnp.int32, x.shape, 0)
    for kstep in range(nsteps):
        sh = 1 << kstep
        xs = jnp.where(row >= sh, pltpu.roll(x, sh, 0), 0.0)
        xsw = pltpu.roll(xs, S5_STATE, 1)
        x = x + apr_ref[kstep:kstep + 1, :] * xs + api_ref[kstep:kstep + 1, :] * xsw
    xprev = jnp.where(row >= 1, pltpu.roll(x, 1, 0), 0.0)
    y = y + _dot(xprev, cm_ref[...], HI)
    o_ref[...] = _gelu_tanh(y + dsk_ref[...] * u)


def _s5_tables(lam_re, lam_im, log_step, b_re, b_im, c_re, c_im, nsteps):
    g, n = lam_re.shape
    blk = S5_BLK
    step = jnp.exp(log_step)[:, None]
    zr, zi = lam_re * step, lam_im * step

    def powers(m):
        m = m.astype(F32)[:, None, None]
        mag = jnp.exp(zr[None] * m)
        return mag * jnp.cos(zi[None] * m), mag * jnp.sin(zi[None] * m)

    ab_re, ab_im = jnp.exp(zr) * jnp.cos(zi), jnp.exp(zr) * jnp.sin(zi)
    den = lam_re * lam_re + lam_im * lam_im
    f_re = ((ab_re - 1.0) * lam_re + ab_im * lam_im) / den
    f_im = (ab_im * lam_re - (ab_re - 1.0) * lam_im) / den
    bb_re = f_re[..., None] * b_re - f_im[..., None] * b_im
    bb_im = f_re[..., None] * b_im + f_im[..., None] * b_re
    pr, pi = powers(jnp.arange(blk + 1))
    cpr = c_re[None] * pr[:, :, None, :] - c_im[None] * pi[:, :, None, :]
    cpi = c_re[None] * pi[:, :, None, :] + c_im[None] * pr[:, :, None, :]
    kern = (jnp.einsum('tgcn,gnd->tgcd', cpr[:blk], bb_re, precision=HI)
            - jnp.einsum('tgcn,gnd->tgcd', cpi[:blk], bb_im, precision=HI))
    tt = jnp.arange(blk)
    diff = tt[None, :] - tt[:, None]
    kst = jnp.where((diff >= 0)[:, :, None, None, None], kern[jnp.clip(diff, 0, blk - 1)], 0.0)
    t0 = kst.transpose(2, 0, 4, 1, 3).reshape(g, blk * S5_GROUP, blk * S5_GROUP)
    pr_b, pi_b = pr[blk - 1 - tt], pi[blk - 1 - tt]
    b_r = pr_b[..., None] * bb_re[None] - pi_b[..., None] * bb_im[None]
    b_i = pr_b[..., None] * bb_im[None] + pi_b[..., None] * bb_re[None]
    bmat = jnp.concatenate([b_r, b_i], axis=2).transpose(1, 0, 3, 2).reshape(g, blk * S5_GROUP, 2 * n)
    cmat = jnp.concatenate([cpr[1:], -cpi[1:]], axis=3)
    cmat = cmat.transpose(1, 3, 0, 2).reshape(g, 2 * n, blk * S5_GROUP)
    ar, ai = powers(float(blk) * (2.0 ** jnp.arange(nsteps)))
    apr = jnp.concatenate([ar, ar], axis=-1).transpose(1, 0, 2)
    api = jnp.concatenate([-ai, ai], axis=-1).transpose(1, 0, 2)
    pad = ((0, 0), (0, 16 - nsteps), (0, 0))
    return t0, bmat, cmat, jnp.pad(apr, pad), jnp.pad(api, pad)


def _s5(su, lam_re, lam_im, log_step, b_re, b_im, c_re, c_im, d_skip):
    t = su.shape[0]
    rows = t // S5_BLK
    nsteps = max(1, int(math.ceil(math.log2(rows))))
    g, w = S5_GROUPS, S5_BLK * S5_GROUP
    t0, bmat, cmat, apr, api = _s5_tables(lam_re, lam_im, log_step, b_re, b_im, c_re, c_im, nsteps)
    dsk = jnp.tile(d_skip.reshape(g, 1, S5_GROUP), (1, 1, S5_BLK))
    u = su.reshape(rows, S5_BLK, g, S5_GROUP).transpose(2, 0, 1, 3).reshape(g, rows, w)
    v = pl.pallas_call(
        functools.partial(_s5_kernel, nsteps=nsteps),
        out_shape=jax.ShapeDtypeStruct((g, rows, w), F32),
        grid=(g,),
        in_specs=[pl.BlockSpec((None, rows, w), lambda i: (i, 0, 0)),
                  pl.BlockSpec((None, w, w), lambda i: (i, 0, 0)),
                  pl.BlockSpec((None, w, 2 * S5_STATE), lambda i: (i, 0, 0)),
                  pl.BlockSpec((None, 2 * S5_STATE, w), lambda i: (i, 0, 0)),
                  pl.BlockSpec((None, 16, 2 * S5_STATE), lambda i: (i, 0, 0)),
                  pl.BlockSpec((None, 16, 2 * S5_STATE), lambda i: (i, 0, 0)),
                  pl.BlockSpec((None, 1, w), lambda i: (i, 0, 0))],
        out_specs=pl.BlockSpec((None, rows, w), lambda i: (i, 0, 0)),
        compiler_params=_cp("arbitrary"),
        name="s5_scan",
    )(u, t0, bmat, cmat, apr, api, dsk)
    return v.reshape(g, rows, S5_BLK, S5_GROUP).transpose(1, 2, 0, 3).reshape(t, S5_CH)


def _ssd_kernel(z_ref, x_ref, b_ref, c_ref, sm_ref, cwx_ref, cwb_ref, cwc_ref, cbx_ref, cbb_ref, cbc_ref,
                exp_ref, dtb_ref, alog_ref, dsk_ref, nw_ref, o_ref, st_ref, tx_ref, tb_ref, tc_ref):
    @pl.when(pl.program_id(0) == 0)
    def _():
        st_ref[...] = jnp.zeros_like(st_ref)
        tx_ref[...] = jnp.zeros_like(tx_ref)
        tb_ref[...] = jnp.zeros_like(tb_ref)
        tc_ref[...] = jnp.zeros_like(tc_ref)

    c = SSD_CHUNK
    xin, bin_, cin = x_ref[...], b_ref[...], c_ref[...]
    xs = _silu(_causal_conv(xin, tx_ref[...], cwx_ref[...]) + cbx_ref[...])
    bs = _silu(_causal_conv(bin_, tb_ref[...], cwb_ref[...]) + cbb_ref[...])
    cs = _silu(_causal_conv(cin, tc_ref[...], cwc_ref[...]) + cbc_ref[...])
    tx_ref[...] = xin[c - 8:, :]
    tb_ref[...] = bin_[c - 8:, :]
    tc_ref[...] = cin[c - 8:, :]

    ii = lax.broadcasted_iota(jnp.int32, (c, c), 0)
    jj = lax.broadcasted_iota(jnp.int32, (c, c), 1)
    lower = ii >= jj
    dt = _softplus(_dot(sm_ref[...], exp_ref[...], HI) + dtb_ref[...])
    acum = _dot(lower.astype(F32), dt * (-jnp.exp(alog_ref[...])), HI)
    xdt = xs * dt
    lane = lax.broadcasted_iota(jnp.int32, (c, 128), 1)
    lo = lane < SSD_HD
    npair = SSD_HEADS // 2
    per_group = npair // SSD_GROUPS
    ys = []
    for gi in range(SSD_GROUPS):
        bg = bs[:, gi * SSD_STATE:(gi + 1) * SSD_STATE]
        cg = cs[:, gi * SSD_STATE:(gi + 1) * SSD_STATE]
        cb = _bdot_nt(cg, bg)
        ss = None
        for pj in range(per_group):
            pi = gi * per_group + pj
            cols = slice(pi * 128, (pi + 1) * 128)
            ac = acum[:, cols]
            xd = xdt[:, cols]
            la = _masked_decay(_pair_diff(ac, lane, 0), lower)
            lb = _masked_decay(_pair_diff(ac, lane, SSD_HD), lower)
            y = _bdot(cb * la, jnp.where(lo, xd, 0.0)) + _bdot(cb * lb, jnp.where(lo, 0.0, xd))
            st = st_ref[pi]
            y = y + _bdot(cg, st) * jnp.exp(ac)
            alast = ac[c - 1:c, :]
            st_ref[pi] = st * jnp.exp(alast) + _bdot_tn(bg, xd * jnp.exp(alast - ac))
            y = (y + dsk_ref[:, cols] * xs[:, cols]) * _silu(z_ref[:, cols])
            sq = jnp.sum(y * y, axis=-1, keepdims=True)
            ss = sq if ss is None else ss + sq
            ys.append(y)
        scale = lax.rsqrt(ss / (SSD_INNER // SSD_GROUPS) + RMS_EPS)
        for pj in range(per_group):
            pi = gi * per_group + pj
            cols = slice(pi * 128, (pi + 1) * 128)
            o_ref[:, cols] = (ys[pi] * scale * nw_ref[:, cols]).astype(o_ref.dtype)


def _ssd(p, conv_w, conv_b, dt_bias, a_log, d_skip, norm_w):
    t = p.shape[0]
    c = SSD_CHUNK
    nbc = SSD_GROUPS * SSD_STATE
    taps = lambda w: jnp.pad(w.T, ((0, 8 - CONV_WIDTH), (0, 0)))
    cwx, cwb, cwc = taps(conv_w[:SSD_INNER]), taps(conv_w[SSD_INNER:SSD_INNER + nbc]), taps(conv_w[SSD_INNER + nbc:])
    cbx = conv_b[:SSD_INNER].reshape(1, -1)
    cbb = conv_b[SSD_INNER:SSD_INNER + nbc].reshape(1, -1)
    cbc = conv_b[SSD_INNER + nbc:].reshape(1, -1)
    expand = np.zeros((128, SSD_INNER), np.float32)
    for h in range(SSD_HEADS):
        expand[16 + h, h * SSD_HD:(h + 1) * SSD_HD] = 1.0
    rep = lambda a: jnp.repeat(a, SSD_HD).reshape(1, SSD_INNER)
    full = lambda n: pl.BlockSpec((1, n), lambda i: (0, 0))
    return pl.pallas_call(
        _ssd_kernel,
        out_shape=jax.ShapeDtypeStruct((t, SSD_INNER), BF16),
        grid=(t // c,),
        in_specs=[pl.BlockSpec((c, SSD_INNER), lambda i: (i, C_MZ // SSD_INNER)),
                  pl.BlockSpec((c, SSD_INNER), lambda i: (i, C_MX // SSD_INNER)),
                  pl.BlockSpec((c, nbc), lambda i: (i, C_MB // nbc)),
                  pl.BlockSpec((c, nbc), lambda i: (i, C_MC // nbc)),
                  pl.BlockSpec((c, 128), lambda i: (i, C_SMALL // 128)),
                  pl.BlockSpec((8, SSD_INNER), lambda i: (0, 0)),
                  pl.BlockSpec((8, nbc), lambda i: (0, 0)),
                  pl.BlockSpec((8, nbc), lambda i: (0, 0)),
                  full(SSD_INNER), full(nbc), full(nbc),
                  pl.BlockSpec((128, SSD_INNER), lambda i: (0, 0)),
                  full(SSD_INNER), full(SSD_INNER), full(SSD_INNER), full(SSD_INNER)],
        out_specs=pl.BlockSpec((c, SSD_INNER), lambda i: (i, 0)),
        scratch_shapes=[pltpu.VMEM((SSD_HEADS // 2, SSD_STATE, 128), F32),
                        pltpu.VMEM((8, SSD_INNER), F32), pltpu.VMEM((8, nbc), F32), pltpu.VMEM((8, nbc), F32)],
        compiler_params=_cp("arbitrary"),
        name="ssd",
    )(p, p, p, p, p, cwx, cwb, cwc, cbx, cbb, cbc, jnp.asarray(expand),
      rep(dt_bias), rep(a_log), rep(d_skip), norm_w.reshape(1, SSD_INNER))


def _reorder_w_in(w):
    pad = jnp.zeros((w.shape[0], NP - w.shape[1]), w.dtype)
    return jnp.concatenate([w[:, :7168], w[:, 7184:10768], w[:, 10784:], w[:, 7168:7184], w[:, 10768:10784], pad],
                           axis=1).astype(BF16)


def kernel(x, positions, w_in, gdn_conv_w, gdn_a_log, gdn_dt_bias, gdn_norm_w, s5_lam_re, s5_lam_im, s5_log_step, s5_b_re, s5_b_im, s5_c_re, s5_c_im, s5_d, s5_glu_w, s5_glu_b, ssd_conv_w, ssd_conv_b, ssd_dt_bias, ssd_a_log, ssd_d, ssd_norm_w, w_branch_ret, w_branch_gdn, w_branch_s5, w_branch_ssd, w_out, ln1_g, ln1_b, w_up, w_down, ln2_g, ln2_b):
    bsz, t, d = x.shape
    assert bsz == 1 and d == D_MODEL
    xf = x.reshape(t, d)
    xb = xf.astype(BF16)
    cos, sin = _rope_tables(positions.reshape(t), min(t, 1024))
    for l in range(DEPTH):
        p = _matmul(xb, _reorder_w_in(w_in[l]), F32, 512, 1024)
        y_ret = _retention(p, cos, sin, 512)
        y_gdn = _gdn(p, gdn_conv_w[l], gdn_a_log[l], gdn_dt_bias[l], gdn_norm_w[l], 256)
        v_s5 = _s5(p[:, C_SU:C_SU + S5_CH], s5_lam_re[l], s5_lam_im[l], s5_log_step[l], s5_b_re[l], s5_b_im[l],
                   s5_c_re[l], s5_c_im[l], s5_d[l])
        y_s5 = _glu(v_s5, s5_glu_w[l].astype(BF16), s5_glu_b[l], 512)
        y_ssd = _ssd(p, ssd_conv_w[l], ssd_conv_b[l], ssd_dt_bias[l], ssd_a_log[l], ssd_d[l], ssd_norm_w[l])
        ws = [w.astype(BF16) for w in (w_branch_ret[l], w_branch_gdn[l], w_branch_s5[l], w_branch_ssd[l])]
        merged = _merge(p, [y_ret, y_gdn, y_s5, y_ssd], ws, 512, 512)
        xf, xb = _matmul_res_ln(merged, w_out[l].astype(BF16), xf, ln1_g[l], ln1_b[l], 256, D_MODEL)
        hid = _matmul(xb, w_up[l].astype(BF16), BF16, 512, 1024, relu2=True)
        xf, xb = _matmul_res_ln(hid, w_down[l].astype(BF16), xf, ln2_g[l], ln2_b[l], 512, 1024)
    return xf.reshape(bsz, t, d)
```

```python
import functools
import math

import numpy as np
import jax
import jax.numpy as jnp
from jax import lax
from jax.experimental import pallas as pl
from jax.experimental.pallas import tpu as pltpu

F32 = jnp.float32
BF16 = jnp.bfloat16
HI = lax.Precision.HIGHEST

D_MODEL = 2048
DEPTH = 2
RET_HEADS, RET_QK, RET_V, RET_CHUNK = 4, 128, 256, 128
ROPE_BASE = 10000.0
GDN_HEADS, GDN_DIM, GDN_CHUNK = 8, 128, 64
CONV_WIDTH = 4
S5_CH, S5_GROUP, S5_STATE = 1024, 16, 64
S5_GROUPS = S5_CH // S5_GROUP
S5_BLK = 16
SSD_HEADS, SSD_HD, SSD_GROUPS, SSD_STATE, SSD_CHUNK = 16, 64, 2, 128, 128
SSD_INNER = SSD_HEADS * SSD_HD
D_FF = 4 * D_MODEL
ALPHA = (2 * DEPTH) ** 0.25
LN_EPS = 1e-5
RMS_EPS = 1e-6

C_RQ, C_RK, C_RV, C_RG = 0, 512, 1024, 2048
C_DQ, C_DK, C_DV, C_DZ = 3072, 4096, 5120, 6144
C_SU, C_MZ, C_MX, C_MB, C_MC = 7168, 8192, 9216, 10240, 10496
C_GATE = 10752
C_SMALL = 18944
SSD_DT_LANE = 16
NP = 19456

VMEM_LIMIT = 52 * 1024 * 1024


def _cp(*sem):
    return pltpu.CompilerParams(dimension_semantics=sem, vmem_limit_bytes=VMEM_LIMIT)


def _dot(a, b, prec=None):
    return lax.dot_general(a, b, (((1,), (0,)), ((), ())), precision=prec, preferred_element_type=F32)


def _dot_nt(a, b, prec=None):
    return lax.dot_general(a, b, (((1,), (1,)), ((), ())), precision=prec, preferred_element_type=F32)


def _dot_tn(a, b, prec=None):
    return lax.dot_general(a, b, (((0,), (0,)), ((), ())), precision=prec, preferred_element_type=F32)


def _bdot(a, b):
    return _dot(a.astype(BF16), b.astype(BF16))


def _bdot_nt(a, b):
    return _dot_nt(a.astype(BF16), b.astype(BF16))


def _bdot_tn(a, b):
    return _dot_tn(a.astype(BF16), b.astype(BF16))


def _silu(x):
    return x * (1.0 / (1.0 + jnp.exp(-x)))


def _sigmoid(x):
    return 1.0 / (1.0 + jnp.exp(-x))


def _softplus(x):
    return jnp.maximum(x, 0.0) + jnp.log1p(jnp.exp(-jnp.abs(x)))


def _masked_decay(diff, lower):
    return jnp.where(lower, jnp.exp(jnp.where(lower, diff, 0.0)), 0.0)


def _pair_diff(col, lane):
    hi = col.astype(BF16).astype(F32)
    r1 = col - hi
    mid = r1.astype(BF16).astype(F32)
    lo = (r1 - mid).astype(BF16).astype(F32)
    a = jnp.where(lane == 0, hi, jnp.where(lane == 1, mid, jnp.where(lane == 2, lo, jnp.where(lane < 6, 1.0, 0.0))))
    b = jnp.where(lane < 3, 1.0, jnp.where(lane == 3, -hi, jnp.where(lane == 4, -mid, jnp.where(lane == 5, -lo, 0.0))))
    return _bdot_nt(a, b)


def _chunk_cumsum(x, chunk):
    row = lax.broadcasted_iota(jnp.int32, x.shape, 0) & (chunk - 1)
    sh = 1
    while sh < chunk:
        x = x + jnp.where(row >= sh, pltpu.roll(x, sh, 0), 0.0)
        sh *= 2
    return x


def _shifted(x, tail, s, row8):
    xs = pltpu.roll(x, s, 0)
    ts = pltpu.roll(tail, s, 0)
    first = jnp.where(row8 < s, ts, xs[:8])
    return jnp.concatenate([first, xs[8:]], axis=0)


def _causal_conv(x, tail, w):
    row8 = lax.broadcasted_iota(jnp.int32, (8, x.shape[1]), 0)
    acc = x * w[3:4, :]
    for s in (1, 2, 3):
        acc = acc + _shifted(x, tail, s, row8) * w[3 - s:4 - s, :]
    return acc


def _mm_kernel(a_ref, b_ref, o_ref, *, relu2):
    acc = jnp.dot(a_ref[...], b_ref[...], preferred_element_type=F32)
    if relu2:
        acc = jnp.square(jnp.maximum(acc, 0.0))
    o_ref[...] = acc.astype(o_ref.dtype)


def _matmul(a, b, out_dtype, tm, tn, relu2=False):
    m, k = a.shape
    n = b.shape[1]
    return pl.pallas_call(
        functools.partial(_mm_kernel, relu2=relu2),
        out_shape=jax.ShapeDtypeStruct((m, n), out_dtype),
        grid=(n // tn, m // tm),
        in_specs=[pl.BlockSpec((tm, k), lambda j, i: (i, 0)),
                  pl.BlockSpec((k, tn), lambda j, i: (0, j))],
        out_specs=pl.BlockSpec((tm, tn), lambda j, i: (i, j)),
        compiler_params=_cp("arbitrary", "arbitrary"),
        name="matmul",
    )(a, b)


def _mm_ln_kernel(a_ref, b_ref, res_ref, g_ref, beta_ref, of_ref, ob_ref, acc_ref, *, nk):
    k = pl.program_id(1)

    @pl.when(k == 0)
    def _():
        acc_ref[...] = jnp.zeros_like(acc_ref)

    acc_ref[...] += jnp.dot(a_ref[...], b_ref[...], preferred_element_type=F32)

    @pl.when(k == nk - 1)
    def _():
        y = ALPHA * res_ref[...] + acc_ref[...]
        mu = jnp.mean(y, axis=-1, keepdims=True)
        d = y - mu
        var = jnp.mean(d * d, axis=-1, keepdims=True)
        out = d * lax.rsqrt(var + LN_EPS) * g_ref[...] + beta_ref[...]
        of_ref[...] = out
        ob_ref[...] = out.astype(BF16)


def _matmul_res_ln(a, b, res, g, beta, tm, tk):
    m, k = a.shape
    n = b.shape[1]
    nk = k // tk
    return pl.pallas_call(
        functools.partial(_mm_ln_kernel, nk=nk),
        out_shape=(jax.ShapeDtypeStruct((m, n), F32), jax.ShapeDtypeStruct((m, n), BF16)),
        grid=(m // tm, nk),
        in_specs=[pl.BlockSpec((tm, tk), lambda i, kk: (i, kk)),
                  pl.BlockSpec((tk, n), lambda i, kk: (kk, 0)),
                  pl.BlockSpec((tm, n), lambda i, kk: (i, 0)),
                  pl.BlockSpec((1, n), lambda i, kk: (0, 0)),
                  pl.BlockSpec((1, n), lambda i, kk: (0, 0))],
        out_specs=(pl.BlockSpec((tm, n), lambda i, kk: (i, 0)),
                   pl.BlockSpec((tm, n), lambda i, kk: (i, 0))),
        scratch_shapes=[pltpu.VMEM((tm, n), F32)],
        compiler_params=_cp("arbitrary", "arbitrary"),
        name="matmul_res_ln",
    )(a, b, res, g.reshape(1, n), beta.reshape(1, n))


def _merge_kernel(y0, y1, y2, y3, w0, w1, w2, w3, g0, g1, g2, g3, o_ref):
    acc = None
    for y_ref, w_ref, g_ref in ((y0, w0, g0), (y1, w1, g1), (y2, w2, g2), (y3, w3, g3)):
        t = _sigmoid(g_ref[...]) * jnp.dot(y_ref[...], w_ref[...], preferred_element_type=F32)
        acc = t if acc is None else acc + t
    o_ref[...] = acc.astype(o_ref.dtype)


def _merge(p, ys, ws, tm, tn):
    t = p.shape[0]
    kb = ys[0].shape[1]
    gate_blk = C_GATE // tn
    per_branch = D_MODEL // tn
    y_specs = [pl.BlockSpec((tm, kb), lambda i, j: (i, 0)) for _ in range(4)]
    w_specs = [pl.BlockSpec((kb, tn), lambda i, j: (0, j)) for _ in range(4)]
    g_specs = [pl.BlockSpec((tm, tn), functools.partial(lambda i, j, b: (i, gate_blk + b * per_branch + j), b=b))
               for b in range(4)]
    return pl.pallas_call(
        _merge_kernel,
        out_shape=jax.ShapeDtypeStruct((t, D_MODEL), BF16),
        grid=(t // tm, D_MODEL // tn),
        in_specs=y_specs + w_specs + g_specs,
        out_specs=pl.BlockSpec((tm, tn), lambda i, j: (i, j)),
        compiler_params=_cp("arbitrary", "arbitrary"),
        name="merge",
    )(*ys, *ws, p, p, p, p)


def _glu_kernel(v_ref, w_ref, b_ref, o_ref):
    v = v_ref[...]
    z = jnp.dot(v.astype(BF16), w_ref[...], preferred_element_type=F32) + b_ref[...]
    o_ref[...] = (v * _sigmoid(z)).astype(o_ref.dtype)


def _glu(v, w, b, tm):
    t, n = v.shape
    return pl.pallas_call(
        _glu_kernel,
        out_shape=jax.ShapeDtypeStruct((t, n), BF16),
        grid=(t // tm,),
        in_specs=[pl.BlockSpec((tm, n), lambda i: (i, 0)),
                  pl.BlockSpec((n, n), lambda i: (0, 0)),
                  pl.BlockSpec((1, n), lambda i: (0, 0))],
        out_specs=pl.BlockSpec((tm, n), lambda i: (i, 0)),
        compiler_params=_cp("arbitrary"),
        name="s5_glu",
    )(v, w, b.reshape(1, n))


def _rope_kernel(pos_ref, inv_ref, cos_ref, sin_ref):
    ang = pos_ref[...].astype(F32) * inv_ref[...]
    lane = lax.broadcasted_iota(jnp.int32, ang.shape, 1)
    s = jnp.sin(ang)
    cos_ref[...] = jnp.cos(ang)
    sin_ref[...] = jnp.where(lane < RET_QK // 2, -s, s)


def _rope_tables(positions, tb):
    t = positions.shape[0]
    half = RET_QK // 2
    inv = (ROPE_BASE ** (-np.arange(half, dtype=np.float64) / half)).astype(np.float32)
    inv = jnp.asarray(np.concatenate([inv, inv])[None, :])
    return pl.pallas_call(
        _rope_kernel,
        out_shape=(jax.ShapeDtypeStruct((t, RET_QK), F32), jax.ShapeDtypeStruct((t, RET_QK), F32)),
        grid=(t // tb,),
        in_specs=[pl.BlockSpec((tb, 1), lambda i: (i, 0)),
                  pl.BlockSpec((1, RET_QK), lambda i: (0, 0))],
        out_specs=(pl.BlockSpec((tb, RET_QK), lambda i: (i, 0)),
                   pl.BlockSpec((tb, RET_QK), lambda i: (i, 0))),
        compiler_params=_cp("arbitrary"),
        name="rope_tables",
    )(positions.reshape(t, 1), inv)


def _ret_kernel(q_ref, k_ref, v_ref, g_ref, cos_ref, sin_ref, dmat_ref, qdec_ref, kdec_ref, gam_ref,
                o_ref, r_ref, *, nchunk):
    @pl.when(pl.program_id(1) == 0)
    def _():
        r_ref[...] = jnp.zeros_like(r_ref)

    c = RET_CHUNK
    dmat = dmat_ref[...]
    qdec = qdec_ref[...]
    kdec = kdec_ref[...]
    gam = gam_ref[0:1, :]
    for ci in range(nchunk):
        rows = slice(ci * c, (ci + 1) * c)
        cos = cos_ref[rows, :]
        sin = sin_ref[rows, :]
        q = q_ref[rows, :]
        k = k_ref[rows, :]
        v = v_ref[rows, :]
        qr = q * cos + pltpu.roll(q, RET_QK // 2, 1) * sin
        kr = (k * cos + pltpu.roll(k, RET_QK // 2, 1) * sin) * (RET_QK ** -0.5)
        r = r_ref[...]
        scores = _bdot_nt(qr, kr) * dmat
        y = _bdot(scores, v) + _bdot(qr * qdec, r)
        r_ref[...] = r * gam + _bdot_tn(kr * kdec, v)
        mu = jnp.mean(y, axis=-1, keepdims=True)
        d = y - mu
        var = jnp.mean(d * d, axis=-1, keepdims=True)
        yn = d * lax.rsqrt(var + LN_EPS)
        o_ref[rows, :] = (_silu(g_ref[rows, :]) * yn).astype(o_ref.dtype)


def _ret_constants():
    c = RET_CHUNK
    h = np.arange(RET_HEADS, dtype=np.float64)
    log_g = np.log(1.0 - 2.0 ** (-5.0 - h))
    idx = np.arange(c, dtype=np.float64)
    diff = idx[:, None] - idx[None, :]
    dmat = np.where(diff >= 0, np.exp(np.where(diff >= 0, diff, 0.0)[None] * log_g[:, None, None]), 0.0)
    qdec = np.exp((idx + 1.0)[None, :] * log_g[:, None])
    kdec = np.exp((c - 1.0 - idx)[None, :] * log_g[:, None])
    gam = np.exp(c * log_g)
    qdec = np.broadcast_to(qdec[:, :, None], (RET_HEADS, c, RET_QK))
    kdec = np.broadcast_to(kdec[:, :, None], (RET_HEADS, c, RET_QK))
    gam = np.broadcast_to(gam[:, None, None], (RET_HEADS, 8, RET_V))
    f = lambda a: jnp.asarray(np.ascontiguousarray(a), dtype=F32)
    return f(dmat), f(qdec), f(kdec), f(gam)


def _retention(p, cos, sin, tb):
    t = p.shape[0]
    dmat, qdec, kdec, gam = _ret_constants()
    c = RET_CHUNK
    return pl.pallas_call(
        functools.partial(_ret_kernel, nchunk=tb // c),
        out_shape=jax.ShapeDtypeStruct((t, RET_HEADS * RET_V), BF16),
        grid=(RET_HEADS, t // tb),
        in_specs=[pl.BlockSpec((tb, RET_QK), lambda h, i: (i, C_RQ // RET_QK + h)),
                  pl.BlockSpec((tb, RET_QK), lambda h, i: (i, C_RK // RET_QK + h)),
                  pl.BlockSpec((tb, RET_V), lambda h, i: (i, C_RV // RET_V + h)),
                  pl.BlockSpec((tb, RET_V), lambda h, i: (i, C_RG // RET_V + h)),
                  pl.BlockSpec((tb, RET_QK), lambda h, i: (i, 0)),
                  pl.BlockSpec((tb, RET_QK), lambda h, i: (i, 0)),
                  pl.BlockSpec((None, c, c), lambda h, i: (h, 0, 0)),
                  pl.BlockSpec((None, c, RET_QK), lambda h, i: (h, 0, 0)),
                  pl.BlockSpec((None, c, RET_QK), lambda h, i: (h, 0, 0)),
                  pl.BlockSpec((None, 8, RET_V), lambda h, i: (h, 0, 0))],
        out_specs=pl.BlockSpec((tb, RET_V), lambda h, i: (i, h)),
        scratch_shapes=[pltpu.VMEM((RET_QK, RET_V), F32)],
        compiler_params=_cp("arbitrary", "arbitrary"),
        name="retention",
    )(p, p, p, p, cos, sin, dmat, qdec, kdec, gam)


GDN_SUB = 16


def _gdn_kernel(q_ref, k_ref, v_ref, z_ref, sm_ref, cw_ref, hp_ref, nw_ref, o_ref,
                s_ref, tq_ref, tk_ref, tv_ref, *, nchunk):
    h = pl.program_id(0)

    @pl.when(pl.program_id(1) == 0)
    def _():
        s_ref[...] = jnp.zeros_like(s_ref)
        tq_ref[...] = jnp.zeros_like(tq_ref)
        tk_ref[...] = jnp.zeros_like(tk_ref)
        tv_ref[...] = jnp.zeros_like(tv_ref)

    c = GDN_CHUNK
    tb = nchunk * c
    xq, xk, xv = q_ref[...], k_ref[...], v_ref[...]
    q = _silu(_causal_conv(xq, tq_ref[...], cw_ref[0]))
    k = _silu(_causal_conv(xk, tk_ref[...], cw_ref[1]))
    v = _silu(_causal_conv(xv, tv_ref[...], cw_ref[2]))
    tq_ref[...] = xq[tb - 8:, :]
    tk_ref[...] = xk[tb - 8:, :]
    tv_ref[...] = xv[tb - 8:, :]
    q = q * lax.rsqrt(jnp.sum(q * q, axis=-1, keepdims=True) + RMS_EPS) * (GDN_DIM ** -0.5)
    k = k * lax.rsqrt(jnp.sum(k * k, axis=-1, keepdims=True) + RMS_EPS)

    sm = sm_ref[...]
    lane_b = lax.broadcasted_iota(jnp.int32, sm.shape, 1)
    g_heads = -jnp.exp(hp_ref[0:1, :]) * _softplus(sm + hp_ref[1:2, :])
    cum_heads = _chunk_cumsum(g_heads, c)
    gcum = jnp.sum(jnp.where(lane_b == h, cum_heads, 0.0), axis=-1, keepdims=True)
    beta = _sigmoid(jnp.sum(jnp.where(lane_b == h + GDN_HEADS, sm, 0.0), axis=-1, keepdims=True))
    eg = jnp.exp(gcum)

    ii = lax.broadcasted_iota(jnp.int32, (c, c), 0)
    jj = lax.broadcasted_iota(jnp.int32, (c, c), 1)
    lower = ii >= jj
    strict = ii > jj
    same_sub = jnp.right_shift(ii, 4) == jnp.right_shift(jj, 4)
    eye = (ii == jj).astype(F32)
    lane = lax.broadcasted_iota(jnp.int32, (c, GDN_DIM), 1)
    nw = nw_ref[...]

    chunks = range(nchunk)
    rows_of = [slice(ci * c, (ci + 1) * c) for ci in chunks]
    qs, ks, vs = [q[r] for r in rows_of], [k[r] for r in rows_of], [v[r] for r in rows_of]
    bs, gs, egs = [beta[r] for r in rows_of], [gcum[r] for r in rows_of], [eg[r] for r in rows_of]
    gammas = [_masked_decay(_pair_diff(gs[ci], lane), lower) for ci in chunks]
    ms = [bs[ci] * _bdot_nt(ks[ci], ks[ci]) * gammas[ci] for ci in chunks]
    qks = [_bdot_nt(qs[ci], ks[ci]) * gammas[ci] for ci in chunks]
    pws = [jnp.where(strict & same_sub, -m, 0.0) for m in ms]
    loffs = [jnp.where(strict & jnp.logical_not(same_sub), m, 0.0) for m in ms]
    dinvs = [eye + pw for pw in pws]
    for _ in range(3):
        pws = [_bdot(pw, pw) for pw in pws]
        dinvs = [dinvs[ci] + _bdot(dinvs[ci], pws[ci]) for ci in chunks]
    es = [-_bdot(dinvs[ci], loffs[ci]) for ci in chunks]
    ys = [_bdot(dinvs[ci], jnp.concatenate([vs[ci] * bs[ci], ks[ci] * (bs[ci] * egs[ci])], axis=1)) for ci in chunks]
    e2s = [_bdot(e, e) for e in es]
    ys = [ys[ci] + _bdot(es[ci], ys[ci]) for ci in chunks]
    ys = [ys[ci] + _bdot(e2s[ci], ys[ci]) for ci in chunks]
    glasts = [g[c - 1:c, :] for g in gs]
    kty = [_bdot_tn(ks[ci] * jnp.exp(glasts[ci] - gs[ci]), ys[ci]) for ci in chunks]
    qky = [_bdot(qks[ci], ys[ci]) for ci in chunks]
    s = s_ref[...]
    for ci in chunks:
        sb = s.astype(BF16)
        s_next = s * jnp.exp(glasts[ci]) + kty[ci][:, :GDN_DIM] - _dot(kty[ci][:, GDN_DIM:].astype(BF16), sb)
        o = _dot((qs[ci] * egs[ci] - qky[ci][:, GDN_DIM:]).astype(BF16), sb) + qky[ci][:, :GDN_DIM]
        s = s_next
        o = o * lax.rsqrt(jnp.mean(o * o, axis=-1, keepdims=True) + RMS_EPS) * nw
        o_ref[rows_of[ci], :] = (o * _silu(z_ref[rows_of[ci], :])).astype(o_ref.dtype)
    s_ref[...] = s


def _gdn(p, conv_w, a_log, dt_bias, norm_w, tb):
    t = p.shape[0]
    hd = GDN_DIM
    assert GDN_CHUNK == 4 * GDN_SUB
    cw = conv_w.reshape(3, GDN_HEADS, hd, CONV_WIDTH).transpose(0, 1, 3, 2)
    cw = jnp.pad(cw, ((0, 0), (0, 0), (0, 8 - CONV_WIDTH), (0, 0)))
    hp = jnp.pad(jnp.stack([a_log, dt_bias], axis=0), ((0, 6), (0, 128 - GDN_HEADS)))
    blk = lambda off: (lambda h, i: (i, off // hd + h))
    return pl.pallas_call(
        functools.partial(_gdn_kernel, nchunk=tb // GDN_CHUNK),
        out_shape=jax.ShapeDtypeStruct((t, GDN_HEADS * hd), BF16),
        grid=(GDN_HEADS, t // tb),
        in_specs=[pl.BlockSpec((tb, hd), blk(C_DQ)),
                  pl.BlockSpec((tb, hd), blk(C_DK)),
                  pl.BlockSpec((tb, hd), blk(C_DV)),
                  pl.BlockSpec((tb, hd), blk(C_DZ)),
                  pl.BlockSpec((tb, 128), lambda h, i: (i, C_SMALL // 128)),
                  pl.BlockSpec((3, None, 8, hd), lambda h, i: (0, h, 0, 0)),
                  pl.BlockSpec((8, 128), lambda h, i: (0, 0)),
                  pl.BlockSpec((1, hd), lambda h, i: (0, 0))],
        out_specs=pl.BlockSpec((tb, hd), lambda h, i: (i, h)),
        scratch_shapes=[pltpu.VMEM((hd, hd), F32), pltpu.VMEM((8, hd), F32),
                        pltpu.VMEM((8, hd), F32), pltpu.VMEM((8, hd), F32)],
        compiler_params=_cp("arbitrary", "arbitrary"),
        name="gated_deltanet",
    )(p, p, p, p, p, cw, hp, norm_w.reshape(1, hd))


def _gelu_tanh(x):
    return 0.5 * x * (1.0 + jnp.tanh(math.sqrt(2.0 / math.pi) * (x + 0.044715 * (x * x * x))))


def _s5_kernel(u_ref, t0_ref, bm_ref, cm_ref, apr_ref, api_ref, dsk_ref, o_ref, *, nsteps):
    u = u_ref[...]
    ub = u.astype(BF16)
    y = _dot(ub, t0_ref[...])
    x = _dot(ub, bm_ref[...])
    row = lax.broadcasted_iota(jnp.int32, x.shape, 0)
    for kstep in range(nsteps):
        sh = 1 << kstep
        xs = jnp.where(row >= sh, pltpu.roll(x, sh, 0), 0.0)
        xsw = pltpu.roll(xs, S5_STATE, 1)
        x = x + apr_ref[kstep:kstep + 1, :] * xs + api_ref[kstep:kstep + 1, :] * xsw
    xprev = jnp.where(row >= 1, pltpu.roll(x, 1, 0), 0.0)
    y = y + _dot(xprev.astype(BF16), cm_ref[...])
    o_ref[...] = _gelu_tanh(y + dsk_ref[...] * u)


def _s5_tables(lam_re, lam_im, log_step, b_re, b_im, c_re, c_im, nsteps):
    g, n = lam_re.shape
    blk = S5_BLK
    step = jnp.exp(log_step)[:, None]
    zr, zi = lam_re * step, lam_im * step

    def powers(m):
        m = m.astype(F32)[:, None, None]
        mag = jnp.exp(zr[None] * m)
        return mag * jnp.cos(zi[None] * m), mag * jnp.sin(zi[None] * m)

    ab_re, ab_im = jnp.exp(zr) * jnp.cos(zi), jnp.exp(zr) * jnp.sin(zi)
    den = lam_re * lam_re + lam_im * lam_im
    f_re = ((ab_re - 1.0) * lam_re + ab_im * lam_im) / den
    f_im = (ab_im * lam_re - (ab_re - 1.0) * lam_im) / den
    bb_re = f_re[..., None] * b_re - f_im[..., None] * b_im
    bb_im = f_re[..., None] * b_im + f_im[..., None] * b_re
    pr, pi = powers(jnp.arange(blk + 1))
    cpr = c_re[None] * pr[:, :, None, :] - c_im[None] * pi[:, :, None, :]
    cpi = c_re[None] * pi[:, :, None, :] + c_im[None] * pr[:, :, None, :]
    kern = (jnp.einsum('tgcn,gnd->tgcd', cpr[:blk], bb_re, precision=HI)
            - jnp.einsum('tgcn,gnd->tgcd', cpi[:blk], bb_im, precision=HI))
    tt = jnp.arange(blk)
    diff = tt[None, :] - tt[:, None]
    kst = jnp.where((diff >= 0)[:, :, None, None, None], kern[jnp.clip(diff, 0, blk - 1)], 0.0)
    t0 = kst.transpose(2, 0, 4, 1, 3).reshape(g, blk * S5_GROUP, blk * S5_GROUP)
    pr_b, pi_b = pr[blk - 1 - tt], pi[blk - 1 - tt]
    b_r = pr_b[..., None] * bb_re[None] - pi_b[..., None] * bb_im[None]
    b_i = pr_b[..., None] * bb_im[None] + pi_b[..., None] * bb_re[None]
    bmat = jnp.concatenate([b_r, b_i], axis=2).transpose(1, 0, 3, 2).reshape(g, blk * S5_GROUP, 2 * n)
    cmat = jnp.concatenate([cpr[1:], -cpi[1:]], axis=3)
    cmat = cmat.transpose(1, 3, 0, 2).reshape(g, 2 * n, blk * S5_GROUP)
    ar, ai = powers(float(blk) * (2.0 ** jnp.arange(nsteps)))
    apr = jnp.concatenate([ar, ar], axis=-1).transpose(1, 0, 2)
    api = jnp.concatenate([-ai, ai], axis=-1).transpose(1, 0, 2)
    pad = ((0, 0), (0, 16 - nsteps), (0, 0))
    return t0.astype(BF16), bmat.astype(BF16), cmat.astype(BF16), jnp.pad(apr, pad), jnp.pad(api, pad)


def _s5(su, lam_re, lam_im, log_step, b_re, b_im, c_re, c_im, d_skip):
    t = su.shape[0]
    rows = t // S5_BLK
    nsteps = max(1, int(math.ceil(math.log2(rows))))
    g, w = S5_GROUPS, S5_BLK * S5_GROUP
    t0, bmat, cmat, apr, api = _s5_tables(lam_re, lam_im, log_step, b_re, b_im, c_re, c_im, nsteps)
    dsk = jnp.tile(d_skip.reshape(g, 1, S5_GROUP), (1, 1, S5_BLK))
    u = su.reshape(rows, S5_BLK, g, S5_GROUP).transpose(2, 0, 1, 3).reshape(g, rows, w)
    v = pl.pallas_call(
        functools.partial(_s5_kernel, nsteps=nsteps),
        out_shape=jax.ShapeDtypeStruct((g, rows, w), F32),
        grid=(g,),
        in_specs=[pl.BlockSpec((None, rows, w), lambda i: (i, 0, 0)),
                  pl.BlockSpec((None, w, w), lambda i: (i, 0, 0)),
                  pl.BlockSpec((None, w, 2 * S5_STATE), lambda i: (i, 0, 0)),
                  pl.BlockSpec((None, 2 * S5_STATE, w), lambda i: (i, 0, 0)),
                  pl.BlockSpec((None, 16, 2 * S5_STATE), lambda i: (i, 0, 0)),
                  pl.BlockSpec((None, 16, 2 * S5_STATE), lambda i: (i, 0, 0)),
                  pl.BlockSpec((None, 1, w), lambda i: (i, 0, 0))],
        out_specs=pl.BlockSpec((None, rows, w), lambda i: (i, 0, 0)),
        compiler_params=_cp("arbitrary"),
        name="s5_scan",
    )(u, t0, bmat, cmat, apr, api, dsk)
    return v.reshape(g, rows, S5_BLK, S5_GROUP).transpose(1, 2, 0, 3).reshape(t, S5_CH)


def _ssd_kernel(z_ref, x_ref, b_ref, c_ref, sm_ref, cwx_ref, cwb_ref, cwc_ref, cbx_ref, cbb_ref, cbc_ref,
                dtb_ref, alog_ref, dsk_ref, nw_ref, o_ref, st_ref, tx_ref, tb_ref, tc_ref):
    @pl.when(pl.program_id(0) == 0)
    def _():
        st_ref[...] = jnp.zeros_like(st_ref)
        tx_ref[...] = jnp.zeros_like(tx_ref)
        tb_ref[...] = jnp.zeros_like(tb_ref)
        tc_ref[...] = jnp.zeros_like(tc_ref)

    c = SSD_CHUNK
    xin, bin_, cin = x_ref[...], b_ref[...], c_ref[...]
    xs = _silu(_causal_conv(xin, tx_ref[...], cwx_ref[...]) + cbx_ref[...])
    bs = _silu(_causal_conv(bin_, tb_ref[...], cwb_ref[...]) + cbb_ref[...])
    cs = _silu(_causal_conv(cin, tc_ref[...], cwc_ref[...]) + cbc_ref[...])
    tx_ref[...] = xin[c - 8:, :]
    tb_ref[...] = bin_[c - 8:, :]
    tc_ref[...] = cin[c - 8:, :]

    ii = lax.broadcasted_iota(jnp.int32, (c, c), 0)
    jj = lax.broadcasted_iota(jnp.int32, (c, c), 1)
    lower = ii >= jj
    dt_heads = _softplus(sm_ref[...] + dtb_ref[...])
    acum_heads = _chunk_cumsum(dt_heads * (-jnp.exp(alog_ref[...])), c)
    lane = lax.broadcasted_iota(jnp.int32, (c, 128), 1)
    lo = lane < SSD_HD
    npair = SSD_HEADS // 2
    per_group = npair // SSD_GROUPS
    pairs = range(npair)
    cols_of = [slice(pi * 128, (pi + 1) * 128) for pi in pairs]
    bgs = [bs[:, gi * SSD_STATE:(gi + 1) * SSD_STATE] for gi in range(SSD_GROUPS)]
    cgs = [cs[:, gi * SSD_STATE:(gi + 1) * SSD_STATE] for gi in range(SSD_GROUPS)]
    cbs = [_bdot_nt(cgs[gi], bgs[gi]) for gi in range(SSD_GROUPS)]
    col = lambda a, h: a[:, SSD_DT_LANE + h:SSD_DT_LANE + h + 1]
    acs = [jnp.where(lo, col(acum_heads, 2 * pi), col(acum_heads, 2 * pi + 1)) for pi in pairs]
    xds = [xs[:, cols_of[pi]] * jnp.where(lo, col(dt_heads, 2 * pi), col(dt_heads, 2 * pi + 1)) for pi in pairs]
    las = [_masked_decay(_pair_diff(col(acum_heads, 2 * pi), lane), lower) for pi in pairs]
    lbs = [_masked_decay(_pair_diff(col(acum_heads, 2 * pi + 1), lane), lower) for pi in pairs]
    sts = [st_ref[pi] for pi in pairs]
    y_off = [_bdot(cgs[pi // per_group], sts[pi]) for pi in pairs]
    y_a = [_bdot(cbs[pi // per_group] * las[pi], jnp.where(lo, xds[pi], 0.0)) for pi in pairs]
    y_b = [_bdot(cbs[pi // per_group] * lbs[pi], jnp.where(lo, 0.0, xds[pi])) for pi in pairs]
    alasts = [ac[c - 1:c, :] for ac in acs]
    st_in = [_bdot_tn(bgs[pi // per_group], xds[pi] * jnp.exp(alasts[pi] - acs[pi])) for pi in pairs]
    ys = []
    for pi in pairs:
        st_ref[pi] = sts[pi] * jnp.exp(alasts[pi]) + st_in[pi]
        y = y_a[pi] + y_b[pi] + y_off[pi] * jnp.exp(acs[pi])
        ys.append((y + dsk_ref[:, cols_of[pi]] * xs[:, cols_of[pi]]) * _silu(z_ref[:, cols_of[pi]]))
    for gi in range(SSD_GROUPS):
        members = range(gi * per_group, (gi + 1) * per_group)
        ss = sum(jnp.sum(ys[pi] * ys[pi], axis=-1, keepdims=True) for pi in members)
        scale = lax.rsqrt(ss / (SSD_INNER // SSD_GROUPS) + RMS_EPS)
        for pi in members:
            o_ref[:, cols_of[pi]] = (ys[pi] * scale * nw_ref[:, cols_of[pi]]).astype(o_ref.dtype)


def _ssd(p, conv_w, conv_b, dt_bias, a_log, d_skip, norm_w):
    t = p.shape[0]
    c = SSD_CHUNK
    nbc = SSD_GROUPS * SSD_STATE
    taps = lambda w: jnp.pad(w.T, ((0, 8 - CONV_WIDTH), (0, 0)))
    cwx, cwb, cwc = taps(conv_w[:SSD_INNER]), taps(conv_w[SSD_INNER:SSD_INNER + nbc]), taps(conv_w[SSD_INNER + nbc:])
    cbx = conv_b[:SSD_INNER].reshape(1, -1)
    cbb = conv_b[SSD_INNER:SSD_INNER + nbc].reshape(1, -1)
    cbc = conv_b[SSD_INNER + nbc:].reshape(1, -1)
    on_dt_lanes = lambda a: jnp.pad(a, (SSD_DT_LANE, 128 - SSD_DT_LANE - SSD_HEADS)).reshape(1, 128)
    full = lambda n: pl.BlockSpec((1, n), lambda i: (0, 0))
    return pl.pallas_call(
        _ssd_kernel,
        out_shape=jax.ShapeDtypeStruct((t, SSD_INNER), BF16),
        grid=(t // c,),
        in_specs=[pl.BlockSpec((c, SSD_INNER), lambda i: (i, C_MZ // SSD_INNER)),
                  pl.BlockSpec((c, SSD_INNER), lambda i: (i, C_MX // SSD_INNER)),
                  pl.BlockSpec((c, nbc), lambda i: (i, C_MB // nbc)),
                  pl.BlockSpec((c, nbc), lambda i: (i, C_MC // nbc)),
                  pl.BlockSpec((c, 128), lambda i: (i, C_SMALL // 128)),
                  pl.BlockSpec((8, SSD_INNER), lambda i: (0, 0)),
                  pl.BlockSpec((8, nbc), lambda i: (0, 0)),
                  pl.BlockSpec((8, nbc), lambda i: (0, 0)),
                  full(SSD_INNER), full(nbc), full(nbc),
                  full(128), full(128), full(SSD_INNER), full(SSD_INNER)],
        out_specs=pl.BlockSpec((c, SSD_INNER), lambda i: (i, 0)),
        scratch_shapes=[pltpu.VMEM((SSD_HEADS // 2, SSD_STATE, 128), F32),
                        pltpu.VMEM((8, SSD_INNER), F32), pltpu.VMEM((8, nbc), F32), pltpu.VMEM((8, nbc), F32)],
        compiler_params=_cp("arbitrary"),
        name="ssd",
    )(p, p, p, p, p, cwx, cwb, cwc, cbx, cbb, cbc, on_dt_lanes(dt_bias), on_dt_lanes(a_log),
      jnp.repeat(d_skip, SSD_HD).reshape(1, SSD_INNER), norm_w.reshape(1, SSD_INNER))


def _reorder_w_in(w):
    pad = jnp.zeros((w.shape[0], NP - w.shape[1]), w.dtype)
    return jnp.concatenate([w[:, :7168], w[:, 7184:10768], w[:, 10784:], w[:, 7168:7184], w[:, 10768:10784], pad],
                           axis=1).astype(BF16)


def kernel(x, positions, w_in, gdn_conv_w, gdn_a_log, gdn_dt_bias, gdn_norm_w, s5_lam_re, s5_lam_im, s5_log_step, s5_b_re, s5_b_im, s5_c_re, s5_c_im, s5_d, s5_glu_w, s5_glu_b, ssd_conv_w, ssd_conv_b, ssd_dt_bias, ssd_a_log, ssd_d, ssd_norm_w, w_branch_ret, w_branch_gdn, w_branch_s5, w_branch_ssd, w_out, ln1_g, ln1_b, w_up, w_down, ln2_g, ln2_b):
    bsz, t, d = x.shape
    assert bsz == 1 and d == D_MODEL
    xf = x.reshape(t, d)
    xb = xf.astype(BF16)
    cos, sin = _rope_tables(positions.reshape(t), min(t, 1024))
    for l in range(DEPTH):
        p = _matmul(xb, _reorder_w_in(w_in[l]), F32, 512, 1024)
        y_ret = _retention(p, cos, sin, 512)
        y_gdn = _gdn(p, gdn_conv_w[l], gdn_a_log[l], gdn_dt_bias[l], gdn_norm_w[l], 512)
        v_s5 = _s5(p[:, C_SU:C_SU + S5_CH], s5_lam_re[l], s5_lam_im[l], s5_log_step[l], s5_b_re[l], s5_b_im[l],
                   s5_c_re[l], s5_c_im[l], s5_d[l])
        y_s5 = _glu(v_s5, s5_glu_w[l].astype(BF16), s5_glu_b[l], 512)
        y_ssd = _ssd(p, ssd_conv_w[l], ssd_conv_b[l], ssd_dt_bias[l], ssd_a_log[l], ssd_d[l], ssd_norm_w[l])
        ws = [w.astype(BF16) for w in (w_branch_ret[l], w_branch_gdn[l], w_branch_s5[l], w_branch_ssd[l])]
        merged = _merge(p, [y_ret, y_gdn, y_s5, y_ssd], ws, 512, 512)
        xf, xb = _matmul_res_ln(merged, w_out[l].astype(BF16), xf, ln1_g[l], ln1_b[l], 256, D_MODEL)
        hid = _matmul(xb, w_up[l].astype(BF16), BF16, 512, 1024, relu2=True)
        xf, xb = _matmul_res_ln(hid, w_down[l].astype(BF16), xf, ln2_g[l], ln2_b[l], 512, 2048)
    return xf.reshape(bsz, t, d)
```

```python
import functools
import math

import numpy as np
import jax
import jax.numpy as jnp
from jax import lax
from jax.experimental import pallas as pl
from jax.experimental.pallas import tpu as pltpu

F32 = jnp.float32
BF16 = jnp.bfloat16
HI = lax.Precision.HIGHEST

D_MODEL = 2048
DEPTH = 2
RET_HEADS, RET_QK, RET_V, RET_CHUNK = 4, 128, 256, 128
ROPE_BASE = 10000.0
GDN_HEADS, GDN_DIM, GDN_CHUNK = 8, 128, 64
CONV_WIDTH = 4
S5_CH, S5_GROUP, S5_STATE = 1024, 16, 64
S5_GROUPS = S5_CH // S5_GROUP
S5_BLK = 16
S5_UNIT = 4
S5_UNIT_STATE = S5_UNIT * S5_STATE
SSD_HEADS, SSD_HD, SSD_GROUPS, SSD_STATE, SSD_CHUNK = 16, 64, 2, 128, 128
SSD_INNER = SSD_HEADS * SSD_HD
D_FF = 4 * D_MODEL
ALPHA = (2 * DEPTH) ** 0.25
LN_EPS = 1e-5
RMS_EPS = 1e-6

C_RQ, C_RK, C_RV, C_RG = 0, 512, 1024, 2048
C_DQ, C_DK, C_DV, C_DZ = 3072, 4096, 5120, 6144
C_SU, C_MZ, C_MX, C_MB, C_MC = 7168, 8192, 9216, 10240, 10496
C_GATE = 10752
C_SMALL = 18944
SSD_DT_LANE = 16
NP = 19456

VMEM_LIMIT = 52 * 1024 * 1024


def _cp(*sem):
    return pltpu.CompilerParams(dimension_semantics=sem, vmem_limit_bytes=VMEM_LIMIT)


def _dot(a, b, prec=None):
    return lax.dot_general(a, b, (((1,), (0,)), ((), ())), precision=prec, preferred_element_type=F32)


def _dot_nt(a, b, prec=None):
    return lax.dot_general(a, b, (((1,), (1,)), ((), ())), precision=prec, preferred_element_type=F32)


def _dot_tn(a, b, prec=None):
    return lax.dot_general(a, b, (((0,), (0,)), ((), ())), precision=prec, preferred_element_type=F32)


def _bdot(a, b):
    return _dot(a.astype(BF16), b.astype(BF16))


def _bdot_nt(a, b):
    return _dot_nt(a.astype(BF16), b.astype(BF16))


def _bdot_tn(a, b):
    return _dot_tn(a.astype(BF16), b.astype(BF16))


def _silu(x):
    return x * (1.0 / (1.0 + jnp.exp(-x)))


def _sigmoid(x):
    return 1.0 / (1.0 + jnp.exp(-x))


def _softplus(x):
    return jnp.maximum(x, 0.0) + jnp.log1p(jnp.exp(-jnp.abs(x)))


def _masked_decay(diff, lower):
    return jnp.where(lower, jnp.exp(jnp.where(lower, diff, 0.0)), 0.0)


def _pair_diff(col, lane):
    hi = col.astype(BF16).astype(F32)
    r1 = col - hi
    mid = r1.astype(BF16).astype(F32)
    lo = (r1 - mid).astype(BF16).astype(F32)
    a = jnp.where(lane == 0, hi, jnp.where(lane == 1, mid, jnp.where(lane == 2, lo, jnp.where(lane < 6, 1.0, 0.0))))
    b = jnp.where(lane < 3, 1.0, jnp.where(lane == 3, -hi, jnp.where(lane == 4, -mid, jnp.where(lane == 5, -lo, 0.0))))
    return _bdot_nt(a, b)


def _chunk_cumsum(x, chunk):
    row = lax.broadcasted_iota(jnp.int32, x.shape, 0) & (chunk - 1)
    sh = 1
    while sh < chunk:
        x = x + jnp.where(row >= sh, pltpu.roll(x, sh, 0), 0.0)
        sh *= 2
    return x


def _shifted(x, tail, s, row8):
    xs = pltpu.roll(x, s, 0)
    ts = pltpu.roll(tail, s, 0)
    first = jnp.where(row8 < s, ts, xs[:8])
    return jnp.concatenate([first, xs[8:]], axis=0)


def _causal_conv(x, tail, w):
    row8 = lax.broadcasted_iota(jnp.int32, (8, x.shape[1]), 0)
    acc = x * w[3:4, :]
    for s in (1, 2, 3):
        acc = acc + _shifted(x, tail, s, row8) * w[3 - s:4 - s, :]
    return acc


W_ROWS = 256


def _up_kernel(a_ref, w_ref, o_ref, wb_ref):
    @pl.when(pl.program_id(1) == 0)
    def _():
        for r in range(0, w_ref.shape[0], W_ROWS):
            wb_ref[r:r + W_ROWS, :] = w_ref[r:r + W_ROWS, :].astype(BF16)

    acc = jnp.dot(a_ref[...], wb_ref[...], preferred_element_type=F32)
    o_ref[...] = jnp.square(jnp.maximum(acc, 0.0)).astype(o_ref.dtype)


def _up_proj(a, w, tm, tn):
    m, k = a.shape
    n = w.shape[1]
    return pl.pallas_call(
        _up_kernel,
        out_shape=jax.ShapeDtypeStruct((m, n), BF16),
        grid=(n // tn, m // tm),
        in_specs=[pl.BlockSpec((tm, k), lambda j, i: (i, 0)),
                  pl.BlockSpec((k, tn), lambda j, i: (0, j))],
        out_specs=pl.BlockSpec((tm, tn), lambda j, i: (i, j)),
        scratch_shapes=[pltpu.VMEM((k, tn), BF16)],
        compiler_params=_cp("arbitrary", "arbitrary"),
        name="up_proj",
    )(a, w)


PROJ_TN = 512
PROJ_T1, PROJ_T2, PROJ_TAIL = C_SU // PROJ_TN, C_GATE // PROJ_TN, C_SMALL // PROJ_TN
D_IN_PROJ = 18976
SRC_DA, SRC_MDT = 7168, 10768


def _proj_kernel(a_ref, wa_ref, wn_ref, o_ref, wb_ref):
    j = pl.program_id(0)
    k = wa_ref.shape[0]
    lane = lax.broadcasted_iota(jnp.int32, (W_ROWS, 128), 1)

    def stage_shifted(sh):
        for r in range(0, k, W_ROWS):
            rows = slice(r, r + W_ROWS)
            ra = pltpu.roll(wa_ref[rows, :], PROJ_TN - sh, 1)
            rn = pltpu.roll(wn_ref[rows, :], 128 - sh, 1)
            wb_ref[rows, :PROJ_TN - 128] = ra[:, :PROJ_TN - 128].astype(BF16)
            wb_ref[rows, PROJ_TN - 128:] = jnp.where(lane < 128 - sh, ra[:, PROJ_TN - 128:], rn).astype(BF16)

    @pl.when(pl.program_id(1) == 0)
    def _():
        @pl.when(j < PROJ_T1)
        def _():
            for r in range(0, k, W_ROWS):
                wb_ref[r:r + W_ROWS, :] = wa_ref[r:r + W_ROWS, :].astype(BF16)

        @pl.when((j >= PROJ_T1) & (j < PROJ_T2))
        def _():
            stage_shifted(16)

        @pl.when((j >= PROJ_T2) & (j < PROJ_TAIL))
        def _():
            stage_shifted(32)

        @pl.when(j == PROJ_TAIL)
        def _():
            for r in range(0, k, W_ROWS):
                rows = slice(r, r + W_ROWS)
                first = jnp.where(lane < 16, wa_ref[rows, :128], jnp.where(lane < 32, wn_ref[rows, :], 0.0))
                wb_ref[rows, :128] = first.astype(BF16)
                wb_ref[rows, 128:] = jnp.zeros((W_ROWS, PROJ_TN - 128), BF16)

    o_ref[...] = jnp.dot(a_ref[...], wb_ref[...], preferred_element_type=F32)


def _proj_in(a, w, tm):
    m, k = a.shape
    assert w.shape[1] == D_IN_PROJ and PROJ_TAIL * PROJ_TN == C_SMALL and NP == (PROJ_TAIL + 1) * PROJ_TN
    assert SRC_DA % PROJ_TN == 0 and SRC_MDT % 128 == 16
    per = PROJ_TN // 128
    wa_idx = lambda j, i: (0, jnp.where(j == PROJ_TAIL, SRC_DA // PROJ_TN, j))
    wn_idx = lambda j, i: (0, jnp.where(j == PROJ_TAIL, SRC_MDT // 128, (j + 1) * per))
    return pl.pallas_call(
        _proj_kernel,
        out_shape=jax.ShapeDtypeStruct((m, NP), F32),
        grid=(NP // PROJ_TN, m // tm),
        in_specs=[pl.BlockSpec((tm, k), lambda j, i: (i, 0)),
                  pl.BlockSpec((k, PROJ_TN), wa_idx),
                  pl.BlockSpec((k, 128), wn_idx)],
        out_specs=pl.BlockSpec((tm, PROJ_TN), lambda j, i: (i, j)),
        scratch_shapes=[pltpu.VMEM((k, PROJ_TN), BF16)],
        compiler_params=_cp("arbitrary", "arbitrary"),
        name="proj_in",
    )(a, w, w)


def _mm_ln_kernel(a_ref, b_ref, res_ref, g_ref, beta_ref, of_ref, ob_ref, acc_ref, *, nk):
    k = pl.program_id(1)

    @pl.when(k == 0)
    def _():
        acc_ref[...] = jnp.zeros_like(acc_ref)

    acc_ref[...] += jnp.dot(a_ref[...], b_ref[...], preferred_element_type=F32)

    @pl.when(k == nk - 1)
    def _():
        y = ALPHA * res_ref[...] + acc_ref[...]
        mu = jnp.mean(y, axis=-1, keepdims=True)
        d = y - mu
        var = jnp.mean(d * d, axis=-1, keepdims=True)
        out = d * lax.rsqrt(var + LN_EPS) * g_ref[...] + beta_ref[...]
        of_ref[...] = out
        ob_ref[...] = out.astype(BF16)


def _matmul_res_ln(a, b, res, g, beta, tm, tk):
    m, k = a.shape
    n = b.shape[1]
    nk = k // tk
    return pl.pallas_call(
        functools.partial(_mm_ln_kernel, nk=nk),
        out_shape=(jax.ShapeDtypeStruct((m, n), F32), jax.ShapeDtypeStruct((m, n), BF16)),
        grid=(m // tm, nk),
        in_specs=[pl.BlockSpec((tm, tk), lambda i, kk: (i, kk)),
                  pl.BlockSpec((tk, n), lambda i, kk: (kk, 0)),
                  pl.BlockSpec((tm, n), lambda i, kk: (i, 0)),
                  pl.BlockSpec((1, n), lambda i, kk: (0, 0)),
                  pl.BlockSpec((1, n), lambda i, kk: (0, 0))],
        out_specs=(pl.BlockSpec((tm, n), lambda i, kk: (i, 0)),
                   pl.BlockSpec((tm, n), lambda i, kk: (i, 0))),
        scratch_shapes=[pltpu.VMEM((tm, n), F32)],
        compiler_params=_cp("arbitrary", "arbitrary"),
        name="matmul_res_ln",
    )(a, b, res, g.reshape(1, n), beta.reshape(1, n))


def _merge_kernel(y0, y1, y2, y3, w0, w1, w2, w3, g0, g1, g2, g3, o_ref):
    acc = None
    for y_ref, w_ref, g_ref in ((y0, w0, g0), (y1, w1, g1), (y2, w2, g2), (y3, w3, g3)):
        t = _sigmoid(g_ref[...]) * jnp.dot(y_ref[...], w_ref[...], preferred_element_type=F32)
        acc = t if acc is None else acc + t
    o_ref[...] = acc.astype(o_ref.dtype)


def _merge(p, ys, ws, tm, tn):
    t = p.shape[0]
    kb = ys[0].shape[1]
    gate_blk = C_GATE // tn
    per_branch = D_MODEL // tn
    y_specs = [pl.BlockSpec((tm, kb), lambda i, j: (i, 0)) for _ in range(4)]
    w_specs = [pl.BlockSpec((kb, tn), lambda i, j: (0, j)) for _ in range(4)]
    g_specs = [pl.BlockSpec((tm, tn), functools.partial(lambda i, j, b: (i, gate_blk + b * per_branch + j), b=b))
               for b in range(4)]
    return pl.pallas_call(
        _merge_kernel,
        out_shape=jax.ShapeDtypeStruct((t, D_MODEL), BF16),
        grid=(t // tm, D_MODEL // tn),
        in_specs=y_specs + w_specs + g_specs,
        out_specs=pl.BlockSpec((tm, tn), lambda i, j: (i, j)),
        compiler_params=_cp("arbitrary", "arbitrary"),
        name="merge",
    )(*ys, *ws, p, p, p, p)


def _glu_kernel(v_ref, w_ref, b_ref, o_ref):
    v = v_ref[...]
    z = jnp.dot(v.astype(BF16), w_ref[...], preferred_element_type=F32) + b_ref[...]
    o_ref[...] = (v * _sigmoid(z)).astype(o_ref.dtype)


def _glu(v, w, b, tm):
    t, n = v.shape
    return pl.pallas_call(
        _glu_kernel,
        out_shape=jax.ShapeDtypeStruct((t, n), BF16),
        grid=(t // tm,),
        in_specs=[pl.BlockSpec((tm, n), lambda i: (i, 0)),
                  pl.BlockSpec((n, n), lambda i: (0, 0)),
                  pl.BlockSpec((1, n), lambda i: (0, 0))],
        out_specs=pl.BlockSpec((tm, n), lambda i: (i, 0)),
        compiler_params=_cp("arbitrary"),
        name="s5_glu",
    )(v, w, b.reshape(1, n))


def _rope_kernel(pos_ref, inv_ref, cos_ref, sin_ref):
    ang = pos_ref[...].astype(F32) * inv_ref[...]
    lane = lax.broadcasted_iota(jnp.int32, ang.shape, 1)
    s = jnp.sin(ang)
    cos_ref[...] = jnp.cos(ang)
    sin_ref[...] = jnp.where(lane < RET_QK // 2, -s, s)


def _rope_tables(positions, tb):
    t = positions.shape[0]
    half = RET_QK // 2
    inv = (ROPE_BASE ** (-np.arange(half, dtype=np.float64) / half)).astype(np.float32)
    inv = jnp.asarray(np.concatenate([inv, inv])[None, :])
    return pl.pallas_call(
        _rope_kernel,
        out_shape=(jax.ShapeDtypeStruct((t, RET_QK), F32), jax.ShapeDtypeStruct((t, RET_QK), F32)),
        grid=(t // tb,),
        in_specs=[pl.BlockSpec((tb, 1), lambda i: (i, 0)),
                  pl.BlockSpec((1, RET_QK), lambda i: (0, 0))],
        out_specs=(pl.BlockSpec((tb, RET_QK), lambda i: (i, 0)),
                   pl.BlockSpec((tb, RET_QK), lambda i: (i, 0))),
        compiler_params=_cp("arbitrary"),
        name="rope_tables",
    )(positions.reshape(t, 1), inv)


def _ret_kernel(q_ref, k_ref, v_ref, g_ref, cos_ref, sin_ref, dmat_ref, qdec_ref, kdec_ref, gam_ref,
                o_ref, r_ref, *, nchunk):
    @pl.when(pl.program_id(1) == 0)
    def _():
        r_ref[...] = jnp.zeros_like(r_ref)

    c = RET_CHUNK
    dmat = dmat_ref[...]
    qdec = qdec_ref[...]
    kdec = kdec_ref[...]
    gam = gam_ref[0:1, :]
    for ci in range(nchunk):
        rows = slice(ci * c, (ci + 1) * c)
        cos = cos_ref[rows, :]
        sin = sin_ref[rows, :]
        q = q_ref[rows, :]
        k = k_ref[rows, :]
        v = v_ref[rows, :]
        qr = q * cos + pltpu.roll(q, RET_QK // 2, 1) * sin
        kr = (k * cos + pltpu.roll(k, RET_QK // 2, 1) * sin) * (RET_QK ** -0.5)
        r = r_ref[...]
        scores = _bdot_nt(qr, kr) * dmat
        y = _bdot(scores, v) + _bdot(qr * qdec, r)
        r_ref[...] = r * gam + _bdot_tn(kr * kdec, v)
        mu = jnp.mean(y, axis=-1, keepdims=True)
        d = y - mu
        var = jnp.mean(d * d, axis=-1, keepdims=True)
        yn = d * lax.rsqrt(var + LN_EPS)
        o_ref[rows, :] = (_silu(g_ref[rows, :]) * yn).astype(o_ref.dtype)


def _ret_constants():
    c = RET_CHUNK
    h = np.arange(RET_HEADS, dtype=np.float64)
    log_g = np.log(1.0 - 2.0 ** (-5.0 - h))
    idx = np.arange(c, dtype=np.float64)
    diff = idx[:, None] - idx[None, :]
    dmat = np.where(diff >= 0, np.exp(np.where(diff >= 0, diff, 0.0)[None] * log_g[:, None, None]), 0.0)
    qdec = np.exp((idx + 1.0)[None, :] * log_g[:, None])
    kdec = np.exp((c - 1.0 - idx)[None, :] * log_g[:, None])
    gam = np.exp(c * log_g)
    qdec = np.broadcast_to(qdec[:, :, None], (RET_HEADS, c, RET_QK))
    kdec = np.broadcast_to(kdec[:, :, None], (RET_HEADS, c, RET_QK))
    gam = np.broadcast_to(gam[:, None, None], (RET_HEADS, 8, RET_V))
    f = lambda a: jnp.asarray(np.ascontiguousarray(a), dtype=F32)
    return f(dmat), f(qdec), f(kdec), f(gam)


def _retention(p, cos, sin, tb):
    t = p.shape[0]
    dmat, qdec, kdec, gam = _ret_constants()
    c = RET_CHUNK
    return pl.pallas_call(
        functools.partial(_ret_kernel, nchunk=tb // c),
        out_shape=jax.ShapeDtypeStruct((t, RET_HEADS * RET_V), BF16),
        grid=(RET_HEADS, t // tb),
        in_specs=[pl.BlockSpec((tb, RET_QK), lambda h, i: (i, C_RQ // RET_QK + h)),
                  pl.BlockSpec((tb, RET_QK), lambda h, i: (i, C_RK // RET_QK + h)),
                  pl.BlockSpec((tb, RET_V), lambda h, i: (i, C_RV // RET_V + h)),
                  pl.BlockSpec((tb, RET_V), lambda h, i: (i, C_RG // RET_V + h)),
                  pl.BlockSpec((tb, RET_QK), lambda h, i: (i, 0)),
                  pl.BlockSpec((tb, RET_QK), lambda h, i: (i, 0)),
                  pl.BlockSpec((None, c, c), lambda h, i: (h, 0, 0)),
                  pl.BlockSpec((None, c, RET_QK), lambda h, i: (h, 0, 0)),
                  pl.BlockSpec((None, c, RET_QK), lambda h, i: (h, 0, 0)),
                  pl.BlockSpec((None, 8, RET_V), lambda h, i: (h, 0, 0))],
        out_specs=pl.BlockSpec((tb, RET_V), lambda h, i: (i, h)),
        scratch_shapes=[pltpu.VMEM((RET_QK, RET_V), F32)],
        compiler_params=_cp("arbitrary", "arbitrary"),
        name="retention",
    )(p, p, p, p, cos, sin, dmat, qdec, kdec, gam)


GDN_SUB = 16


def _gdn_kernel(q_ref, k_ref, v_ref, z_ref, sm_ref, cw_ref, hp_ref, nw_ref, o_ref,
                s_ref, tq_ref, tk_ref, tv_ref, *, nchunk):
    h = pl.program_id(0)

    @pl.when(pl.program_id(1) == 0)
    def _():
        s_ref[...] = jnp.zeros_like(s_ref)
        tq_ref[...] = jnp.zeros_like(tq_ref)
        tk_ref[...] = jnp.zeros_like(tk_ref)
        tv_ref[...] = jnp.zeros_like(tv_ref)

    c = GDN_CHUNK
    tb = nchunk * c
    xq, xk, xv = q_ref[...], k_ref[...], v_ref[...]
    q = _silu(_causal_conv(xq, tq_ref[...], cw_ref[0]))
    k = _silu(_causal_conv(xk, tk_ref[...], cw_ref[1]))
    v = _silu(_causal_conv(xv, tv_ref[...], cw_ref[2]))
    tq_ref[...] = xq[tb - 8:, :]
    tk_ref[...] = xk[tb - 8:, :]
    tv_ref[...] = xv[tb - 8:, :]
    q = q * lax.rsqrt(jnp.sum(q * q, axis=-1, keepdims=True) + RMS_EPS) * (GDN_DIM ** -0.5)
    k = k * lax.rsqrt(jnp.sum(k * k, axis=-1, keepdims=True) + RMS_EPS)

    sm = sm_ref[...]
    lane_b = lax.broadcasted_iota(jnp.int32, sm.shape, 1)
    g_heads = -jnp.exp(hp_ref[0:1, :]) * _softplus(sm + hp_ref[1:2, :])
    cum_heads = _chunk_cumsum(g_heads, c)
    gcum = jnp.sum(jnp.where(lane_b == h, cum_heads, 0.0), axis=-1, keepdims=True)
    beta = _sigmoid(jnp.sum(jnp.where(lane_b == h + GDN_HEADS, sm, 0.0), axis=-1, keepdims=True))
    eg = jnp.exp(gcum)

    ii = lax.broadcasted_iota(jnp.int32, (c, c), 0)
    jj = lax.broadcasted_iota(jnp.int32, (c, c), 1)
    lower = ii >= jj
    strict = ii > jj
    same_sub = jnp.right_shift(ii, 4) == jnp.right_shift(jj, 4)
    eye = (ii == jj).astype(F32)
    lane = lax.broadcasted_iota(jnp.int32, (c, GDN_DIM), 1)
    nw = nw_ref[...]

    chunks = range(nchunk)
    rows_of = [slice(ci * c, (ci + 1) * c) for ci in chunks]
    qs, ks, vs = [q[r] for r in rows_of], [k[r] for r in rows_of], [v[r] for r in rows_of]
    bs, gs, egs = [beta[r] for r in rows_of], [gcum[r] for r in rows_of], [eg[r] for r in rows_of]
    gammas = [_masked_decay(_pair_diff(gs[ci], lane), lower) for ci in chunks]
    ms = [bs[ci] * _bdot_nt(ks[ci], ks[ci]) * gammas[ci] for ci in chunks]
    qks = [_bdot_nt(qs[ci], ks[ci]) * gammas[ci] for ci in chunks]
    pws = [jnp.where(strict & same_sub, -m, 0.0) for m in ms]
    loffs = [jnp.where(strict & jnp.logical_not(same_sub), m, 0.0) for m in ms]
    dinvs = [eye + pw for pw in pws]
    for _ in range(3):
        pws = [_bdot(pw, pw) for pw in pws]
        dinvs = [dinvs[ci] + _bdot(dinvs[ci], pws[ci]) for ci in chunks]
    es = [-_bdot(dinvs[ci], loffs[ci]) for ci in chunks]
    ys = [_bdot(dinvs[ci], jnp.concatenate([vs[ci] * bs[ci], ks[ci] * (bs[ci] * egs[ci])], axis=1)) for ci in chunks]
    e2s = [_bdot(e, e) for e in es]
    ys = [ys[ci] + _bdot(es[ci], ys[ci]) for ci in chunks]
    ys = [ys[ci] + _bdot(e2s[ci], ys[ci]) for ci in chunks]
    glasts = [g[c - 1:c, :] for g in gs]
    kty = [_bdot_tn(ks[ci] * jnp.exp(glasts[ci] - gs[ci]), ys[ci]) for ci in chunks]
    qky = [_bdot(qks[ci], ys[ci]) for ci in chunks]
    s = s_ref[...]
    for ci in chunks:
        sb = s.astype(BF16)
        s_next = s * jnp.exp(glasts[ci]) + kty[ci][:, :GDN_DIM] - _dot(kty[ci][:, GDN_DIM:].astype(BF16), sb)
        o = _dot((qs[ci] * egs[ci] - qky[ci][:, GDN_DIM:]).astype(BF16), sb) + qky[ci][:, :GDN_DIM]
        s = s_next
        o = o * lax.rsqrt(jnp.mean(o * o, axis=-1, keepdims=True) + RMS_EPS) * nw
        o_ref[rows_of[ci], :] = (o * _silu(z_ref[rows_of[ci], :])).astype(o_ref.dtype)
    s_ref[...] = s


def _gdn(p, conv_w, a_log, dt_bias, norm_w, tb):
    t = p.shape[0]
    hd = GDN_DIM
    assert GDN_CHUNK == 4 * GDN_SUB
    cw = conv_w.reshape(3, GDN_HEADS, hd, CONV_WIDTH).transpose(0, 1, 3, 2)
    cw = jnp.pad(cw, ((0, 0), (0, 0), (0, 8 - CONV_WIDTH), (0, 0)))
    hp = jnp.pad(jnp.stack([a_log, dt_bias], axis=0), ((0, 6), (0, 128 - GDN_HEADS)))
    blk = lambda off: (lambda h, i: (i, off // hd + h))
    return pl.pallas_call(
        functools.partial(_gdn_kernel, nchunk=tb // GDN_CHUNK),
        out_shape=jax.ShapeDtypeStruct((t, GDN_HEADS * hd), BF16),
        grid=(GDN_HEADS, t // tb),
        in_specs=[pl.BlockSpec((tb, hd), blk(C_DQ)),
                  pl.BlockSpec((tb, hd), blk(C_DK)),
                  pl.BlockSpec((tb, hd), blk(C_DV)),
                  pl.BlockSpec((tb, hd), blk(C_DZ)),
                  pl.BlockSpec((tb, 128), lambda h, i: (i, C_SMALL // 128)),
                  pl.BlockSpec((3, None, 8, hd), lambda h, i: (0, h, 0, 0)),
                  pl.BlockSpec((8, 128), lambda h, i: (0, 0)),
                  pl.BlockSpec((1, hd), lambda h, i: (0, 0))],
        out_specs=pl.BlockSpec((tb, hd), lambda h, i: (i, h)),
        scratch_shapes=[pltpu.VMEM((hd, hd), F32), pltpu.VMEM((8, hd), F32),
                        pltpu.VMEM((8, hd), F32), pltpu.VMEM((8, hd), F32)],
        compiler_params=_cp("arbitrary", "arbitrary"),
        name="gated_deltanet",
    )(p, p, p, p, p, cw, hp, norm_w.reshape(1, hd))


def _gelu_tanh(x):
    return 0.5 * x * (1.0 + jnp.tanh(math.sqrt(2.0 / math.pi) * (x + 0.044715 * (x * x * x))))


def _s5_kernel(u_ref, t0_ref, bm_ref, cm_ref, apr_ref, api_ref, dsk_ref, o_ref, *, nsteps):
    u = u_ref[...]
    ub = u.astype(BF16)
    y = _dot(ub, t0_ref[...])
    x = _dot(ub, bm_ref[...])
    row = lax.broadcasted_iota(jnp.int32, x.shape, 0)
    for kstep in range(nsteps):
        sh = 1 << kstep
        xs = jnp.where(row >= sh, pltpu.roll(x, sh, 0), 0.0)
        xsw = pltpu.roll(xs, S5_UNIT_STATE, 1)
        x = x + apr_ref[kstep:kstep + 1, :] * xs + api_ref[kstep:kstep + 1, :] * xsw
    xprev = jnp.where(row >= 1, pltpu.roll(x, 1, 0), 0.0)
    y = y + _dot(xprev.astype(BF16), cm_ref[...])
    o_ref[...] = _gelu_tanh(y + dsk_ref[...] * u)


def _s5_tables(lam_re, lam_im, log_step, b_re, b_im, c_re, c_im, nsteps):
    g, n = lam_re.shape
    blk, qg, cg = S5_BLK, S5_UNIT, S5_GROUP
    nq = g // qg
    step = jnp.exp(log_step)[:, None]
    zr, zi = lam_re * step, lam_im * step

    def powers(m):
        m = jnp.asarray(m, F32)[:, None, None]
        mag = jnp.exp(zr[None] * m)
        return mag * jnp.cos(zi[None] * m), mag * jnp.sin(zi[None] * m)

    ab_re, ab_im = jnp.exp(zr) * jnp.cos(zi), jnp.exp(zr) * jnp.sin(zi)
    den = lam_re * lam_re + lam_im * lam_im
    f_re = ((ab_re - 1.0) * lam_re + ab_im * lam_im) / den
    f_im = (ab_im * lam_re - (ab_re - 1.0) * lam_im) / den
    bb_re = f_re[..., None] * b_re - f_im[..., None] * b_im
    bb_im = f_re[..., None] * b_im + f_im[..., None] * b_re
    pr, pi = powers(np.arange(blk + 1))
    cpr = c_re[None] * pr[:, :, None, :] - c_im[None] * pi[:, :, None, :]
    cpi = c_re[None] * pi[:, :, None, :] + c_im[None] * pr[:, :, None, :]
    kern = (jnp.einsum('tgcn,gnd->tgcd', cpr[:blk], bb_re, precision=HI)
            - jnp.einsum('tgcn,gnd->tgcd', cpi[:blk], bb_im, precision=HI))
    zero = jnp.zeros_like(kern)
    kst = jnp.stack([jnp.concatenate([zero[:s], kern[:blk - s]], axis=0) for s in range(blk)])
    eye = jnp.eye(qg, dtype=F32)
    t0 = jnp.einsum('stqgcd,gh->qsgdthc', kst.reshape(blk, blk, nq, qg, cg, cg), eye)
    t0 = t0.reshape(nq, blk * qg * cg, blk * qg * cg)
    pr_b, pi_b = pr[:blk][::-1], pi[:blk][::-1]
    b_r = pr_b[..., None] * bb_re[None] - pi_b[..., None] * bb_im[None]
    b_i = pr_b[..., None] * bb_im[None] + pi_b[..., None] * bb_re[None]
    spread_b = lambda b: jnp.einsum('sqgnd,gh->qsgdhn', b.reshape(blk, nq, qg, n, cg), eye).reshape(
        nq, blk * qg * cg, qg * n)
    bmat = jnp.concatenate([spread_b(b_r), spread_b(b_i)], axis=-1)
    spread_c = lambda c: jnp.einsum('tqgcn,gh->qgnthc', c.reshape(blk, nq, qg, cg, n), eye).reshape(
        nq, qg * n, blk * qg * cg)
    cmat = jnp.concatenate([spread_c(cpr[1:]), spread_c(-cpi[1:])], axis=1)
    ar, ai = powers(float(blk) * (2.0 ** np.arange(nsteps)))
    ar, ai = ar.reshape(nsteps, nq, qg * n), ai.reshape(nsteps, nq, qg * n)
    apr = jnp.concatenate([ar, ar], axis=-1).transpose(1, 0, 2)
    api = jnp.concatenate([-ai, ai], axis=-1).transpose(1, 0, 2)
    pad = ((0, 0), (0, 16 - nsteps), (0, 0))
    return t0.astype(BF16), bmat.astype(BF16), cmat.astype(BF16), jnp.pad(apr, pad), jnp.pad(api, pad)


def _s5(su, lam_re, lam_im, log_step, b_re, b_im, c_re, c_im, d_skip):
    t = su.shape[0]
    rows = t // S5_BLK
    nsteps = max(1, int(math.ceil(math.log2(rows))))
    nq, uw = S5_GROUPS // S5_UNIT, S5_UNIT * S5_GROUP
    w, sw = S5_BLK * uw, 2 * S5_UNIT_STATE
    t0, bmat, cmat, apr, api = _s5_tables(lam_re, lam_im, log_step, b_re, b_im, c_re, c_im, nsteps)
    dsk = jnp.tile(d_skip.reshape(nq, 1, uw), (1, 1, S5_BLK))
    u = su.reshape(rows, S5_BLK, nq, uw).transpose(2, 0, 1, 3).reshape(nq, rows, w)
    v = pl.pallas_call(
        functools.partial(_s5_kernel, nsteps=nsteps),
        out_shape=jax.ShapeDtypeStruct((nq, rows, w), F32),
        grid=(nq,),
        in_specs=[pl.BlockSpec((None, rows, w), lambda i: (i, 0, 0)),
                  pl.BlockSpec((None, w, w), lambda i: (i, 0, 0)),
                  pl.BlockSpec((None, w, sw), lambda i: (i, 0, 0)),
                  pl.BlockSpec((None, sw, w), lambda i: (i, 0, 0)),
                  pl.BlockSpec((None, 16, sw), lambda i: (i, 0, 0)),
                  pl.BlockSpec((None, 16, sw), lambda i: (i, 0, 0)),
                  pl.BlockSpec((None, 1, w), lambda i: (i, 0, 0))],
        out_specs=pl.BlockSpec((None, rows, w), lambda i: (i, 0, 0)),
        compiler_params=_cp("arbitrary"),
        name="s5_scan",
    )(u, t0, bmat, cmat, apr, api, dsk)
    return v.reshape(nq, rows, S5_BLK, uw).transpose(1, 2, 0, 3).reshape(t, S5_CH)


def _ssd_kernel(z_ref, x_ref, b_ref, c_ref, sm_ref, cwx_ref, cwb_ref, cwc_ref, cbx_ref, cbb_ref, cbc_ref,
                dtb_ref, alog_ref, dsk_ref, nw_ref, o_ref, st_ref, tx_ref, tb_ref, tc_ref):
    @pl.when(pl.program_id(0) == 0)
    def _():
        st_ref[...] = jnp.zeros_like(st_ref)
        tx_ref[...] = jnp.zeros_like(tx_ref)
        tb_ref[...] = jnp.zeros_like(tb_ref)
        tc_ref[...] = jnp.zeros_like(tc_ref)

    c = SSD_CHUNK
    xin, bin_, cin = x_ref[...], b_ref[...], c_ref[...]
    xs = _silu(_causal_conv(xin, tx_ref[...], cwx_ref[...]) + cbx_ref[...])
    bs = _silu(_causal_conv(bin_, tb_ref[...], cwb_ref[...]) + cbb_ref[...])
    cs = _silu(_causal_conv(cin, tc_ref[...], cwc_ref[...]) + cbc_ref[...])
    tx_ref[...] = xin[c - 8:, :]
    tb_ref[...] = bin_[c - 8:, :]
    tc_ref[...] = cin[c - 8:, :]

    ii = lax.broadcasted_iota(jnp.int32, (c, c), 0)
    jj = lax.broadcasted_iota(jnp.int32, (c, c), 1)
    lower = ii >= jj
    dt_heads = _softplus(sm_ref[...] + dtb_ref[...])
    acum_heads = _chunk_cumsum(dt_heads * (-jnp.exp(alog_ref[...])), c)
    lane = lax.broadcasted_iota(jnp.int32, (c, 128), 1)
    lo = lane < SSD_HD
    npair = SSD_HEADS // 2
    per_group = npair // SSD_GROUPS
    pairs = range(npair)
    cols_of = [slice(pi * 128, (pi + 1) * 128) for pi in pairs]
    bgs = [bs[:, gi * SSD_STATE:(gi + 1) * SSD_STATE] for gi in range(SSD_GROUPS)]
    cgs = [cs[:, gi * SSD_STATE:(gi + 1) * SSD_STATE] for gi in range(SSD_GROUPS)]
    cbs = [_bdot_nt(cgs[gi], bgs[gi]) for gi in range(SSD_GROUPS)]
    col = lambda a, h: a[:, SSD_DT_LANE + h:SSD_DT_LANE + h + 1]
    acs = [jnp.where(lo, col(acum_heads, 2 * pi), col(acum_heads, 2 * pi + 1)) for pi in pairs]
    xds = [xs[:, cols_of[pi]] * jnp.where(lo, col(dt_heads, 2 * pi), col(dt_heads, 2 * pi + 1)) for pi in pairs]
    las = [_masked_decay(_pair_diff(col(acum_heads, 2 * pi), lane), lower) for pi in pairs]
    lbs = [_masked_decay(_pair_diff(col(acum_heads, 2 * pi + 1), lane), lower) for pi in pairs]
    sts = [st_ref[pi] for pi in pairs]
    y_off = [_bdot(cgs[pi // per_group], sts[pi]) for pi in pairs]
    y_a = [_bdot(cbs[pi // per_group] * las[pi], jnp.where(lo, xds[pi], 0.0)) for pi in pairs]
    y_b = [_bdot(cbs[pi // per_group] * lbs[pi], jnp.where(lo, 0.0, xds[pi])) for pi in pairs]
    alasts = [ac[c - 1:c, :] for ac in acs]
    st_in = [_bdot_tn(bgs[pi // per_group], xds[pi] * jnp.exp(alasts[pi] - acs[pi])) for pi in pairs]
    ys = []
    for pi in pairs:
        st_ref[pi] = sts[pi] * jnp.exp(alasts[pi]) + st_in[pi]
        y = y_a[pi] + y_b[pi] + y_off[pi] * jnp.exp(acs[pi])
        ys.append((y + dsk_ref[:, cols_of[pi]] * xs[:, cols_of[pi]]) * _silu(z_ref[:, cols_of[pi]]))
    for gi in range(SSD_GROUPS):
        members = range(gi * per_group, (gi + 1) * per_group)
        ss = sum(jnp.sum(ys[pi] * ys[pi], axis=-1, keepdims=True) for pi in members)
        scale = lax.rsqrt(ss / (SSD_INNER // SSD_GROUPS) + RMS_EPS)
        for pi in members:
            o_ref[:, cols_of[pi]] = (ys[pi] * scale * nw_ref[:, cols_of[pi]]).astype(o_ref.dtype)


def _ssd(p, conv_w, conv_b, dt_bias, a_log, d_skip, norm_w):
    t = p.shape[0]
    c = SSD_CHUNK
    nbc = SSD_GROUPS * SSD_STATE
    taps = lambda w: jnp.pad(w.T, ((0, 8 - CONV_WIDTH), (0, 0)))
    cwx, cwb, cwc = taps(conv_w[:SSD_INNER]), taps(conv_w[SSD_INNER:SSD_INNER + nbc]), taps(conv_w[SSD_INNER + nbc:])
    cbx = conv_b[:SSD_INNER].reshape(1, -1)
    cbb = conv_b[SSD_INNER:SSD_INNER + nbc].reshape(1, -1)
    cbc = conv_b[SSD_INNER + nbc:].reshape(1, -1)
    on_dt_lanes = lambda a: jnp.pad(a, (SSD_DT_LANE, 128 - SSD_DT_LANE - SSD_HEADS)).reshape(1, 128)
    full = lambda n: pl.BlockSpec((1, n), lambda i: (0, 0))
    return pl.pallas_call(
        _ssd_kernel,
        out_shape=jax.ShapeDtypeStruct((t, SSD_INNER), BF16),
        grid=(t // c,),
        in_specs=[pl.BlockSpec((c, SSD_INNER), lambda i: (i, C_MZ // SSD_INNER)),
                  pl.BlockSpec((c, SSD_INNER), lambda i: (i, C_MX // SSD_INNER)),
                  pl.BlockSpec((c, nbc), lambda i: (i, C_MB // nbc)),
                  pl.BlockSpec((c, nbc), lambda i: (i, C_MC // nbc)),
                  pl.BlockSpec((c, 128), lambda i: (i, C_SMALL // 128)),
                  pl.BlockSpec((8, SSD_INNER), lambda i: (0, 0)),
                  pl.BlockSpec((8, nbc), lambda i: (0, 0)),
                  pl.BlockSpec((8, nbc), lambda i: (0, 0)),
                  full(SSD_INNER), full(nbc), full(nbc),
                  full(128), full(128), full(SSD_INNER), full(SSD_INNER)],
        out_specs=pl.BlockSpec((c, SSD_INNER), lambda i: (i, 0)),
        scratch_shapes=[pltpu.VMEM((SSD_HEADS // 2, SSD_STATE, 128), F32),
                        pltpu.VMEM((8, SSD_INNER), F32), pltpu.VMEM((8, nbc), F32), pltpu.VMEM((8, nbc), F32)],
        compiler_params=_cp("arbitrary"),
        name="ssd",
    )(p, p, p, p, p, cwx, cwb, cwc, cbx, cbb, cbc, on_dt_lanes(dt_bias), on_dt_lanes(a_log),
      jnp.repeat(d_skip, SSD_HD).reshape(1, SSD_INNER), norm_w.reshape(1, SSD_INNER))


def kernel(x, positions, w_in, gdn_conv_w, gdn_a_log, gdn_dt_bias, gdn_norm_w, s5_lam_re, s5_lam_im, s5_log_step, s5_b_re, s5_b_im, s5_c_re, s5_c_im, s5_d, s5_glu_w, s5_glu_b, ssd_conv_w, ssd_conv_b, ssd_dt_bias, ssd_a_log, ssd_d, ssd_norm_w, w_branch_ret, w_branch_gdn, w_branch_s5, w_branch_ssd, w_out, ln1_g, ln1_b, w_up, w_down, ln2_g, ln2_b):
    bsz, t, d = x.shape
    assert bsz == 1 and d == D_MODEL
    xf = x.reshape(t, d)
    xb = xf.astype(BF16)
    cos, sin = _rope_tables(positions.reshape(t), min(t, 1024))
    for l in range(DEPTH):
        p = _proj_in(xb, w_in[l], 1024)
        y_ret = _retention(p, cos, sin, 512)
        y_gdn = _gdn(p, gdn_conv_w[l], gdn_a_log[l], gdn_dt_bias[l], gdn_norm_w[l], 512)
        v_s5 = _s5(p[:, C_SU:C_SU + S5_CH], s5_lam_re[l], s5_lam_im[l], s5_log_step[l], s5_b_re[l], s5_b_im[l],
                   s5_c_re[l], s5_c_im[l], s5_d[l])
        y_s5 = _glu(v_s5, s5_glu_w[l].astype(BF16), s5_glu_b[l], 512)
        y_ssd = _ssd(p, ssd_conv_w[l], ssd_conv_b[l], ssd_dt_bias[l], ssd_a_log[l], ssd_d[l], ssd_norm_w[l])
        ws = [w.astype(BF16) for w in (w_branch_ret[l], w_branch_gdn[l], w_branch_s5[l], w_branch_ssd[l])]
        merged = _merge(p, [y_ret, y_gdn, y_s5, y_ssd], ws, 1024, 512)
        xf, xb = _matmul_res_ln(merged, w_out[l].astype(BF16), xf, ln1_g[l], ln1_b[l], 512, D_MODEL)
        hid = _up_proj(xb, w_up[l], 1024, 1024)
        xf, xb = _matmul_res_ln(hid, w_down[l].astype(BF16), xf, ln2_g[l], ln2_b[l], 512, 2048)
    return xf.reshape(bsz, t, d)
```

```python
import functools
import math

import numpy as np
import jax
import jax.numpy as jnp
from jax import lax
from jax.experimental import pallas as pl
from jax.experimental.pallas import tpu as pltpu

F32 = jnp.float32
BF16 = jnp.bfloat16
HI = lax.Precision.HIGHEST

D_MODEL = 2048
DEPTH = 2
RET_HEADS, RET_QK, RET_V, RET_CHUNK = 4, 128, 256, 128
ROPE_BASE = 10000.0
GDN_HEADS, GDN_DIM, GDN_CHUNK = 8, 128, 64
CONV_WIDTH = 4
S5_CH, S5_GROUP, S5_STATE = 1024, 16, 64
S5_GROUPS = S5_CH // S5_GROUP
S5_BLK = 16
S5_OCT = 8
S5_OCT_STATE = S5_OCT * S5_STATE
SSD_HEADS, SSD_HD, SSD_GROUPS, SSD_STATE, SSD_CHUNK = 16, 64, 2, 128, 128
SSD_INNER = SSD_HEADS * SSD_HD
D_FF = 4 * D_MODEL
ALPHA = (2 * DEPTH) ** 0.25
LN_EPS = 1e-5
RMS_EPS = 1e-6

C_RQ, C_RK, C_RV, C_RG = 0, 512, 1024, 2048
C_DQ, C_DK, C_DV, C_DZ = 3072, 4096, 5120, 6144
C_SU, C_MZ, C_MX, C_MB, C_MC = 7168, 8192, 9216, 10240, 10496
C_GATE = 10752
C_SMALL = 18944
SSD_DT_LANE = 16
NP = 19456

VMEM_LIMIT = 52 * 1024 * 1024


def _cp(*sem):
    return pltpu.CompilerParams(dimension_semantics=sem, vmem_limit_bytes=VMEM_LIMIT)


def _dot(a, b, prec=None):
    return lax.dot_general(a, b, (((1,), (0,)), ((), ())), precision=prec, preferred_element_type=F32)


def _dot_nt(a, b, prec=None):
    return lax.dot_general(a, b, (((1,), (1,)), ((), ())), precision=prec, preferred_element_type=F32)


def _dot_tn(a, b, prec=None):
    return lax.dot_general(a, b, (((0,), (0,)), ((), ())), precision=prec, preferred_element_type=F32)


def _bdot(a, b):
    return _dot(a.astype(BF16), b.astype(BF16))


def _bdot_nt(a, b):
    return _dot_nt(a.astype(BF16), b.astype(BF16))


def _bdot_tn(a, b):
    return _dot_tn(a.astype(BF16), b.astype(BF16))


def _silu(x):
    return x * (1.0 / (1.0 + jnp.exp(-x)))


def _sigmoid(x):
    return 1.0 / (1.0 + jnp.exp(-x))


def _softplus(x):
    return jnp.maximum(x, 0.0) + jnp.log1p(jnp.exp(-jnp.abs(x)))


def _masked_decay(diff, lower):
    return jnp.where(lower, jnp.exp(jnp.where(lower, diff, 0.0)), 0.0)


def _pair_diff(col, lane):
    hi = col.astype(BF16).astype(F32)
    r1 = col - hi
    mid = r1.astype(BF16).astype(F32)
    lo = (r1 - mid).astype(BF16).astype(F32)
    a = jnp.where(lane == 0, hi, jnp.where(lane == 1, mid, jnp.where(lane == 2, lo, jnp.where(lane < 6, 1.0, 0.0))))
    b = jnp.where(lane < 3, 1.0, jnp.where(lane == 3, -hi, jnp.where(lane == 4, -mid, jnp.where(lane == 5, -lo, 0.0))))
    return _bdot_nt(a, b)


def _chunk_cumsum(x, chunk):
    row = lax.broadcasted_iota(jnp.int32, x.shape, 0) & (chunk - 1)
    sh = 1
    while sh < chunk:
        x = x + jnp.where(row >= sh, pltpu.roll(x, sh, 0), 0.0)
        sh *= 2
    return x


def _shifted(x, tail, s, row8):
    xs = pltpu.roll(x, s, 0)
    ts = pltpu.roll(tail, s, 0)
    first = jnp.where(row8 < s, ts, xs[:8])
    return jnp.concatenate([first, xs[8:]], axis=0)


def _causal_conv(x, tail, w):
    row8 = lax.broadcasted_iota(jnp.int32, (8, x.shape[1]), 0)
    acc = x * w[3:4, :]
    for s in (1, 2, 3):
        acc = acc + _shifted(x, tail, s, row8) * w[3 - s:4 - s, :]
    return acc


def _stage_first_tile(w0_ref, wb_ref, chunk):
    for r in range(0, w0_ref.shape[0], chunk):
        wb_ref[0, r:r + chunk, :] = w0_ref[r:r + chunk, :].astype(BF16)


def _up_kernel(a_ref, w0_ref, wn_ref, o_ref, wb_ref, *, chunk):
    j, i = pl.program_id(0), pl.program_id(1)

    @pl.when((j == 0) & (i == 0))
    def _():
        _stage_first_tile(w0_ref, wb_ref, chunk)

    rows = pl.ds(pl.multiple_of(i * chunk, chunk), chunk)
    wb_ref[(j + 1) % 2, rows, :] = wn_ref[rows, :].astype(BF16)
    acc = jnp.dot(a_ref[...], wb_ref[j % 2], preferred_element_type=F32)
    o_ref[...] = jnp.square(jnp.maximum(acc, 0.0)).astype(o_ref.dtype)


def _up_proj(a, w, tm, tn):
    m, k = a.shape
    n = w.shape[1]
    nj, ni = n // tn, m // tm
    chunk = k // ni
    assert chunk * ni == k and chunk % 16 == 0
    return pl.pallas_call(
        functools.partial(_up_kernel, chunk=chunk),
        out_shape=jax.ShapeDtypeStruct((m, n), BF16),
        grid=(nj, ni),
        in_specs=[pl.BlockSpec((tm, k), lambda j, i: (i, 0)),
                  pl.BlockSpec((k, tn), lambda j, i: (0, 0)),
                  pl.BlockSpec((k, tn), lambda j, i: (0, jnp.minimum(j + 1, nj - 1)))],
        out_specs=pl.BlockSpec((tm, tn), lambda j, i: (i, j)),
        scratch_shapes=[pltpu.VMEM((2, k, tn), BF16)],
        compiler_params=_cp("arbitrary", "arbitrary"),
        name="up_proj",
    )(a, w, w)


PROJ_TN = 512
PROJ_T1, PROJ_T2, PROJ_TAIL = C_SU // PROJ_TN, C_GATE // PROJ_TN, C_SMALL // PROJ_TN
D_IN_PROJ = 18976
SRC_DA, SRC_MDT = 7168, 10768


def _proj_kernel(a_ref, w0_ref, wa_ref, wn_ref, o_ref, wb_ref, *, chunk):
    j, i = pl.program_id(0), pl.program_id(1)
    jn = j + 1
    body = PROJ_TN - 128

    @pl.when((j == 0) & (i == 0))
    def _():
        _stage_first_tile(w0_ref, wb_ref, chunk)

    rows = pl.ds(pl.multiple_of(i * chunk, chunk), chunk)
    lane = lax.broadcasted_iota(jnp.int32, (chunk, 128), 1)
    sh = jnp.where(jn < PROJ_T1, 0, jnp.where(jn < PROJ_T2, 16, 32))
    ra = pltpu.roll(wa_ref[rows, :], jnp.where(sh == 0, 0, PROJ_TN - sh), 1)
    rn = pltpu.roll(wn_ref[rows, :], jnp.where(sh == 0, 0, 128 - sh), 1)
    wb_ref[jn % 2, rows, :body] = ra[:, :body].astype(BF16)
    wb_ref[jn % 2, rows, body:] = jnp.where(lane < 128 - sh, ra[:, body:], rn).astype(BF16)

    @pl.when(jn == PROJ_TAIL)
    def _():
        first = jnp.where(lane < 16, wa_ref[rows, :128], jnp.where(lane < 32, wn_ref[rows, :], 0.0))
        wb_ref[jn % 2, rows, :128] = first.astype(BF16)
        wb_ref[jn % 2, rows, 128:] = jnp.zeros((chunk, body), BF16)

    o_ref[...] = jnp.dot(a_ref[...], wb_ref[j % 2], preferred_element_type=F32)


def _proj_in(a, w, tm):
    m, k = a.shape
    assert w.shape[1] == D_IN_PROJ and PROJ_TAIL * PROJ_TN == C_SMALL and NP == (PROJ_TAIL + 1) * PROJ_TN
    assert SRC_DA % PROJ_TN == 0 and SRC_MDT % 128 == 16 and PROJ_T1 >= 1
    ni = m // tm
    chunk = k // ni
    assert chunk * ni == k and chunk % 16 == 0
    per = PROJ_TN // 128
    nxt = lambda j: jnp.minimum(j + 1, PROJ_TAIL)
    wa_idx = lambda j, i: (0, jnp.where(nxt(j) == PROJ_TAIL, SRC_DA // PROJ_TN, nxt(j)))
    wn_idx = lambda j, i: (0, jnp.where(nxt(j) == PROJ_TAIL, SRC_MDT // 128, (nxt(j) + 1) * per))
    return pl.pallas_call(
        functools.partial(_proj_kernel, chunk=chunk),
        out_shape=jax.ShapeDtypeStruct((m, NP), F32),
        grid=(NP // PROJ_TN, ni),
        in_specs=[pl.BlockSpec((tm, k), lambda j, i: (i, 0)),
                  pl.BlockSpec((k, PROJ_TN), lambda j, i: (0, 0)),
                  pl.BlockSpec((k, PROJ_TN), wa_idx),
                  pl.BlockSpec((k, 128), wn_idx)],
        out_specs=pl.BlockSpec((tm, PROJ_TN), lambda j, i: (i, j)),
        scratch_shapes=[pltpu.VMEM((2, k, PROJ_TN), BF16)],
        compiler_params=_cp("arbitrary", "arbitrary"),
        name="proj_in",
    )(a, w, w, w)


def _mm_ln_kernel(a_ref, b_ref, res_ref, g_ref, beta_ref, of_ref, ob_ref, acc_ref, *, nk):
    k = pl.program_id(1)

    @pl.when(k == 0)
    def _():
        acc_ref[...] = jnp.zeros_like(acc_ref)

    acc_ref[...] += jnp.dot(a_ref[...], b_ref[...], preferred_element_type=F32)

    @pl.when(k == nk - 1)
    def _():
        y = ALPHA * res_ref[...] + acc_ref[...]
        mu = jnp.mean(y, axis=-1, keepdims=True)
        d = y - mu
        var = jnp.mean(d * d, axis=-1, keepdims=True)
        out = d * lax.rsqrt(var + LN_EPS) * g_ref[...] + beta_ref[...]
        of_ref[...] = out
        ob_ref[...] = out.astype(BF16)


def _matmul_res_ln(a, b, res, g, beta, tm, tk):
    m, k = a.shape
    n = b.shape[1]
    nk = k // tk
    return pl.pallas_call(
        functools.partial(_mm_ln_kernel, nk=nk),
        out_shape=(jax.ShapeDtypeStruct((m, n), F32), jax.ShapeDtypeStruct((m, n), BF16)),
        grid=(m // tm, nk),
        in_specs=[pl.BlockSpec((tm, tk), lambda i, kk: (i, kk)),
                  pl.BlockSpec((tk, n), lambda i, kk: (kk, 0)),
                  pl.BlockSpec((tm, n), lambda i, kk: (i, 0)),
                  pl.BlockSpec((1, n), lambda i, kk: (0, 0)),
                  pl.BlockSpec((1, n), lambda i, kk: (0, 0))],
        out_specs=(pl.BlockSpec((tm, n), lambda i, kk: (i, 0)),
                   pl.BlockSpec((tm, n), lambda i, kk: (i, 0))),
        scratch_shapes=[pltpu.VMEM((tm, n), F32)],
        compiler_params=_cp("arbitrary", "arbitrary"),
        name="matmul_res_ln",
    )(a, b, res, g.reshape(1, n), beta.reshape(1, n))


def _merge_kernel(y0, y1, y2, y3, w0, w1, w2, w3, g0, g1, g2, g3, o_ref):
    acc = None
    for y_ref, w_ref, g_ref in ((y0, w0, g0), (y1, w1, g1), (y2, w2, g2), (y3, w3, g3)):
        t = _sigmoid(g_ref[...]) * jnp.dot(y_ref[...], w_ref[...], preferred_element_type=F32)
        acc = t if acc is None else acc + t
    o_ref[...] = acc.astype(o_ref.dtype)


def _merge(p, ys, ws, tm, tn):
    t = p.shape[0]
    kb = ys[0].shape[1]
    gate_blk = C_GATE // tn
    per_branch = D_MODEL // tn
    y_specs = [pl.BlockSpec((tm, kb), lambda i, j: (i, 0)) for _ in range(4)]
    w_specs = [pl.BlockSpec((kb, tn), lambda i, j: (0, j)) for _ in range(4)]
    g_specs = [pl.BlockSpec((tm, tn), functools.partial(lambda i, j, b: (i, gate_blk + b * per_branch + j), b=b))
               for b in range(4)]
    return pl.pallas_call(
        _merge_kernel,
        out_shape=jax.ShapeDtypeStruct((t, D_MODEL), BF16),
        grid=(t // tm, D_MODEL // tn),
        in_specs=y_specs + w_specs + g_specs,
        out_specs=pl.BlockSpec((tm, tn), lambda i, j: (i, j)),
        compiler_params=_cp("arbitrary", "arbitrary"),
        name="merge",
    )(*ys, *ws, p, p, p, p)


def _glu_kernel(v_ref, w_ref, b_ref, o_ref):
    v = v_ref[...]
    z = jnp.dot(v.astype(BF16), w_ref[...], preferred_element_type=F32) + b_ref[...]
    o_ref[...] = (v * _sigmoid(z)).astype(o_ref.dtype)


def _glu(v, w, b, tm):
    t, n = v.shape
    return pl.pallas_call(
        _glu_kernel,
        out_shape=jax.ShapeDtypeStruct((t, n), BF16),
        grid=(t // tm,),
        in_specs=[pl.BlockSpec((tm, n), lambda i: (i, 0)),
                  pl.BlockSpec((n, n), lambda i: (0, 0)),
                  pl.BlockSpec((1, n), lambda i: (0, 0))],
        out_specs=pl.BlockSpec((tm, n), lambda i: (i, 0)),
        compiler_params=_cp("arbitrary"),
        name="s5_glu",
    )(v, w, b.reshape(1, n))


def _rope_kernel(pos_ref, inv_ref, cos_ref, sin_ref):
    ang = pos_ref[...].astype(F32) * inv_ref[...]
    lane = lax.broadcasted_iota(jnp.int32, ang.shape, 1)
    s = jnp.sin(ang)
    cos_ref[...] = jnp.cos(ang)
    sin_ref[...] = jnp.where(lane < RET_QK // 2, -s, s)


def _rope_tables(positions, tb):
    t = positions.shape[0]
    half = RET_QK // 2
    inv = (ROPE_BASE ** (-np.arange(half, dtype=np.float64) / half)).astype(np.float32)
    inv = jnp.asarray(np.concatenate([inv, inv])[None, :])
    return pl.pallas_call(
        _rope_kernel,
        out_shape=(jax.ShapeDtypeStruct((t, RET_QK), F32), jax.ShapeDtypeStruct((t, RET_QK), F32)),
        grid=(t // tb,),
        in_specs=[pl.BlockSpec((tb, 1), lambda i: (i, 0)),
                  pl.BlockSpec((1, RET_QK), lambda i: (0, 0))],
        out_specs=(pl.BlockSpec((tb, RET_QK), lambda i: (i, 0)),
                   pl.BlockSpec((tb, RET_QK), lambda i: (i, 0))),
        compiler_params=_cp("arbitrary"),
        name="rope_tables",
    )(positions.reshape(t, 1), inv)


def _ret_kernel(q_ref, k_ref, v_ref, g_ref, cos_ref, sin_ref, dmat_ref, qdec_ref, kdec_ref, gam_ref,
                o_ref, r_ref, *, nchunk):
    @pl.when(pl.program_id(1) == 0)
    def _():
        r_ref[...] = jnp.zeros_like(r_ref)

    c = RET_CHUNK
    dmat = dmat_ref[...]
    qdec = qdec_ref[...]
    kdec = kdec_ref[...]
    gam = gam_ref[0:1, :]
    for ci in range(nchunk):
        rows = slice(ci * c, (ci + 1) * c)
        cos = cos_ref[rows, :]
        sin = sin_ref[rows, :]
        q = q_ref[rows, :]
        k = k_ref[rows, :]
        v = v_ref[rows, :]
        qr = q * cos + pltpu.roll(q, RET_QK // 2, 1) * sin
        kr = (k * cos + pltpu.roll(k, RET_QK // 2, 1) * sin) * (RET_QK ** -0.5)
        r = r_ref[...]
        scores = _bdot_nt(qr, kr) * dmat
        y = _bdot(scores, v) + _bdot(qr * qdec, r)
        r_ref[...] = r * gam + _bdot_tn(kr * kdec, v)
        mu = jnp.mean(y, axis=-1, keepdims=True)
        d = y - mu
        var = jnp.mean(d * d, axis=-1, keepdims=True)
        yn = d * lax.rsqrt(var + LN_EPS)
        o_ref[rows, :] = (_silu(g_ref[rows, :]) * yn).astype(o_ref.dtype)


def _ret_constants():
    c = RET_CHUNK
    h = np.arange(RET_HEADS, dtype=np.float64)
    log_g = np.log(1.0 - 2.0 ** (-5.0 - h))
    idx = np.arange(c, dtype=np.float64)
    diff = idx[:, None] - idx[None, :]
    dmat = np.where(diff >= 0, np.exp(np.where(diff >= 0, diff, 0.0)[None] * log_g[:, None, None]), 0.0)
    qdec = np.exp((idx + 1.0)[None, :] * log_g[:, None])
    kdec = np.exp((c - 1.0 - idx)[None, :] * log_g[:, None])
    gam = np.exp(c * log_g)
    qdec = np.broadcast_to(qdec[:, :, None], (RET_HEADS, c, RET_QK))
    kdec = np.broadcast_to(kdec[:, :, None], (RET_HEADS, c, RET_QK))
    gam = np.broadcast_to(gam[:, None, None], (RET_HEADS, 8, RET_V))
    f = lambda a: jnp.asarray(np.ascontiguousarray(a), dtype=F32)
    return f(dmat), f(qdec), f(kdec), f(gam)


def _retention(p, cos, sin, tb):
    t = p.shape[0]
    dmat, qdec, kdec, gam = _ret_constants()
    c = RET_CHUNK
    return pl.pallas_call(
        functools.partial(_ret_kernel, nchunk=tb // c),
        out_shape=jax.ShapeDtypeStruct((t, RET_HEADS * RET_V), BF16),
        grid=(RET_HEADS, t // tb),
        in_specs=[pl.BlockSpec((tb, RET_QK), lambda h, i: (i, C_RQ // RET_QK + h)),
                  pl.BlockSpec((tb, RET_QK), lambda h, i: (i, C_RK // RET_QK + h)),
                  pl.BlockSpec((tb, RET_V), lambda h, i: (i, C_RV // RET_V + h)),
                  pl.BlockSpec((tb, RET_V), lambda h, i: (i, C_RG // RET_V + h)),
                  pl.BlockSpec((tb, RET_QK), lambda h, i: (i, 0)),
                  pl.BlockSpec((tb, RET_QK), lambda h, i: (i, 0)),
                  pl.BlockSpec((None, c, c), lambda h, i: (h, 0, 0)),
                  pl.BlockSpec((None, c, RET_QK), lambda h, i: (h, 0, 0)),
                  pl.BlockSpec((None, c, RET_QK), lambda h, i: (h, 0, 0)),
                  pl.BlockSpec((None, 8, RET_V), lambda h, i: (h, 0, 0))],
        out_specs=pl.BlockSpec((tb, RET_V), lambda h, i: (i, h)),
        scratch_shapes=[pltpu.VMEM((RET_QK, RET_V), F32)],
        compiler_params=_cp("arbitrary", "arbitrary"),
        name="retention",
    )(p, p, p, p, cos, sin, dmat, qdec, kdec, gam)


GDN_SUB = 16


def _gdn_kernel(q_ref, k_ref, v_ref, z_ref, sm_ref, cw_ref, hp_ref, nw_ref, o_ref,
                s_ref, tq_ref, tk_ref, tv_ref, *, nchunk):
    h = pl.program_id(0)

    @pl.when(pl.program_id(1) == 0)
    def _():
        s_ref[...] = jnp.zeros_like(s_ref)
        tq_ref[...] = jnp.zeros_like(tq_ref)
        tk_ref[...] = jnp.zeros_like(tk_ref)
        tv_ref[...] = jnp.zeros_like(tv_ref)

    c = GDN_CHUNK
    tb = nchunk * c
    xq, xk, xv = q_ref[...], k_ref[...], v_ref[...]
    q = _silu(_causal_conv(xq, tq_ref[...], cw_ref[0]))
    k = _silu(_causal_conv(xk, tk_ref[...], cw_ref[1]))
    v = _silu(_causal_conv(xv, tv_ref[...], cw_ref[2]))
    tq_ref[...] = xq[tb - 8:, :]
    tk_ref[...] = xk[tb - 8:, :]
    tv_ref[...] = xv[tb - 8:, :]
    q = q * lax.rsqrt(jnp.sum(q * q, axis=-1, keepdims=True) + RMS_EPS) * (GDN_DIM ** -0.5)
    k = k * lax.rsqrt(jnp.sum(k * k, axis=-1, keepdims=True) + RMS_EPS)

    sm = sm_ref[...]
    lane_b = lax.broadcasted_iota(jnp.int32, sm.shape, 1)
    g_heads = -jnp.exp(hp_ref[0:1, :]) * _softplus(sm + hp_ref[1:2, :])
    cum_heads = _chunk_cumsum(g_heads, c)
    gcum = jnp.sum(jnp.where(lane_b == h, cum_heads, 0.0), axis=-1, keepdims=True)
    beta = _sigmoid(jnp.sum(jnp.where(lane_b == h + GDN_HEADS, sm, 0.0), axis=-1, keepdims=True))
    eg = jnp.exp(gcum)

    ii = lax.broadcasted_iota(jnp.int32, (c, c), 0)
    jj = lax.broadcasted_iota(jnp.int32, (c, c), 1)
    lower = ii >= jj
    strict = ii > jj
    same_sub = jnp.right_shift(ii, 4) == jnp.right_shift(jj, 4)
    eye = (ii == jj).astype(F32)
    lane = lax.broadcasted_iota(jnp.int32, (c, GDN_DIM), 1)
    nw = nw_ref[...]

    chunks = range(nchunk)
    rows_of = [slice(ci * c, (ci + 1) * c) for ci in chunks]
    qs, ks, vs = [q[r] for r in rows_of], [k[r] for r in rows_of], [v[r] for r in rows_of]
    bs, gs, egs = [beta[r] for r in rows_of], [gcum[r] for r in rows_of], [eg[r] for r in rows_of]
    gammas = [_masked_decay(_pair_diff(gs[ci], lane), lower) for ci in chunks]
    ms = [bs[ci] * _bdot_nt(ks[ci], ks[ci]) * gammas[ci] for ci in chunks]
    qks = [_bdot_nt(qs[ci], ks[ci]) * gammas[ci] for ci in chunks]
    pws = [jnp.where(strict & same_sub, -m, 0.0) for m in ms]
    loffs = [jnp.where(strict & jnp.logical_not(same_sub), m, 0.0) for m in ms]
    dinvs = [eye + pw for pw in pws]
    for _ in range(3):
        pws = [_bdot(pw, pw) for pw in pws]
        dinvs = [dinvs[ci] + _bdot(dinvs[ci], pws[ci]) for ci in chunks]
    es = [-_bdot(dinvs[ci], loffs[ci]) for ci in chunks]
    ys = [_bdot(dinvs[ci], jnp.concatenate([vs[ci] * bs[ci], ks[ci] * (bs[ci] * egs[ci])], axis=1)) for ci in chunks]
    e2s = [_bdot(e, e) for e in es]
    ys = [ys[ci] + _bdot(es[ci], ys[ci]) for ci in chunks]
    ys = [ys[ci] + _bdot(e2s[ci], ys[ci]) for ci in chunks]
    glasts = [g[c - 1:c, :] for g in gs]
    kty = [_bdot_tn(ks[ci] * jnp.exp(glasts[ci] - gs[ci]), ys[ci]) for ci in chunks]
    qky = [_bdot(qks[ci], ys[ci]) for ci in chunks]
    s = s_ref[...]
    for ci in chunks:
        sb = s.astype(BF16)
        s_next = s * jnp.exp(glasts[ci]) + kty[ci][:, :GDN_DIM] - _dot(kty[ci][:, GDN_DIM:].astype(BF16), sb)
        o = _dot((qs[ci] * egs[ci] - qky[ci][:, GDN_DIM:]).astype(BF16), sb) + qky[ci][:, :GDN_DIM]
        s = s_next
        o = o * lax.rsqrt(jnp.mean(o * o, axis=-1, keepdims=True) + RMS_EPS) * nw
        o_ref[rows_of[ci], :] = (o * _silu(z_ref[rows_of[ci], :])).astype(o_ref.dtype)
    s_ref[...] = s


def _gdn(p, conv_w, a_log, dt_bias, norm_w, tb):
    t = p.shape[0]
    hd = GDN_DIM
    assert GDN_CHUNK == 4 * GDN_SUB
    cw = conv_w.reshape(3, GDN_HEADS, hd, CONV_WIDTH).transpose(0, 1, 3, 2)
    cw = jnp.pad(cw, ((0, 0), (0, 0), (0, 8 - CONV_WIDTH), (0, 0)))
    hp = jnp.pad(jnp.stack([a_log, dt_bias], axis=0), ((0, 6), (0, 128 - GDN_HEADS)))
    blk = lambda off: (lambda h, i: (i, off // hd + h))
    return pl.pallas_call(
        functools.partial(_gdn_kernel, nchunk=tb // GDN_CHUNK),
        out_shape=jax.ShapeDtypeStruct((t, GDN_HEADS * hd), BF16),
        grid=(GDN_HEADS, t // tb),
        in_specs=[pl.BlockSpec((tb, hd), blk(C_DQ)),
                  pl.BlockSpec((tb, hd), blk(C_DK)),
                  pl.BlockSpec((tb, hd), blk(C_DV)),
                  pl.BlockSpec((tb, hd), blk(C_DZ)),
                  pl.BlockSpec((tb, 128), lambda h, i: (i, C_SMALL // 128)),
                  pl.BlockSpec((3, None, 8, hd), lambda h, i: (0, h, 0, 0)),
                  pl.BlockSpec((8, 128), lambda h, i: (0, 0)),
                  pl.BlockSpec((1, hd), lambda h, i: (0, 0))],
        out_specs=pl.BlockSpec((tb, hd), lambda h, i: (i, h)),
        scratch_shapes=[pltpu.VMEM((hd, hd), F32), pltpu.VMEM((8, hd), F32),
                        pltpu.VMEM((8, hd), F32), pltpu.VMEM((8, hd), F32)],
        compiler_params=_cp("arbitrary", "arbitrary"),
        name="gated_deltanet",
    )(p, p, p, p, p, cw, hp, norm_w.reshape(1, hd))


def _gelu_tanh(x):
    return 0.5 * x * (1.0 + jnp.tanh(math.sqrt(2.0 / math.pi) * (x + 0.044715 * (x * x * x))))


def _crot(x, p_re, p_im):
    return p_re * x + p_im * pltpu.roll(x, S5_OCT_STATE, 1)


def _s5_kernel(u_ref, kc_ref, bb_ref, cb_ref, pbr_ref, pbi_ref, pcr_ref, pci_ref, apr_ref, api_ref, dsk_ref,
               o_ref, y_ref, *, rows, nsteps):
    blk = S5_BLK
    us = [u_ref[pl.ds(s, rows, stride=blk), :] for s in range(blk)]
    ubs = [u.astype(BF16) for u in us]
    for s in range(blk):
        w = _dot(ubs[s], kc_ref[:, :(blk - s) * 128])
        if s == 0:
            y_ref[...] = w
        else:
            y_ref[:, s * 128:] += w
    x = None
    for s in range(blk):
        term = _crot(_dot(ubs[s], bb_ref[...]), pbr_ref[s:s + 1, :], pbi_ref[s:s + 1, :])
        x = term if x is None else x + term
    row = lax.broadcasted_iota(jnp.int32, x.shape, 0)
    for kstep in range(nsteps):
        sh = 1 << kstep
        xs = jnp.where(row >= sh, pltpu.roll(x, sh, 0), 0.0)
        x = x + _crot(xs, apr_ref[kstep:kstep + 1, :], api_ref[kstep:kstep + 1, :])
    xprev = jnp.where(row >= 1, pltpu.roll(x, 1, 0), 0.0)
    for t in range(blk):
        carried = _crot(xprev, pcr_ref[t:t + 1, :], pci_ref[t:t + 1, :]).astype(BF16)
        y = y_ref[:, t * 128:(t + 1) * 128] + _dot(carried, cb_ref[...])
        o_ref[pl.ds(t, rows, stride=blk), :] = _gelu_tanh(y + dsk_ref[...] * us[t])


def _s5_tables(lam_re, lam_im, log_step, b_re, b_im, c_re, c_im, nsteps):
    g, n = lam_re.shape
    blk, qg, cg = S5_BLK, S5_OCT, S5_GROUP
    nq = g // qg
    step = jnp.exp(log_step)[:, None]
    zr, zi = lam_re * step, lam_im * step

    def powers(m):
        m = jnp.asarray(m, F32)[:, None, None]
        mag = jnp.exp(zr[None] * m)
        return mag * jnp.cos(zi[None] * m), mag * jnp.sin(zi[None] * m)

    ab_re, ab_im = jnp.exp(zr) * jnp.cos(zi), jnp.exp(zr) * jnp.sin(zi)
    den = lam_re * lam_re + lam_im * lam_im
    f_re = ((ab_re - 1.0) * lam_re + ab_im * lam_im) / den
    f_im = (ab_im * lam_re - (ab_re - 1.0) * lam_im) / den
    bb_re = f_re[..., None] * b_re - f_im[..., None] * b_im
    bb_im = f_re[..., None] * b_im + f_im[..., None] * b_re
    pr, pi = powers(np.arange(blk + 1))
    cpr = c_re[None] * pr[:, :, None, :] - c_im[None] * pi[:, :, None, :]
    cpi = c_re[None] * pi[:, :, None, :] + c_im[None] * pr[:, :, None, :]
    kern = (jnp.einsum('tgcn,gnd->tgcd', cpr[:blk], bb_re, precision=HI)
            - jnp.einsum('tgcn,gnd->tgcd', cpi[:blk], bb_im, precision=HI))
    eye = jnp.eye(qg, dtype=F32)
    kc = jnp.einsum('tqgcd,gh->qgdthc', kern.reshape(blk, nq, qg, cg, cg), eye).reshape(nq, qg * cg, blk * qg * cg)
    spread_b = lambda b: jnp.einsum('qgnd,gh->qgdhn', b.reshape(nq, qg, n, cg), eye).reshape(nq, qg * cg, qg * n)
    bbase = jnp.concatenate([spread_b(bb_re), spread_b(bb_im)], axis=-1)
    spread_c = lambda c: jnp.einsum('qgcn,gh->qgnhc', c.reshape(nq, qg, cg, n), eye).reshape(nq, qg * n, qg * cg)
    cbase = jnp.concatenate([spread_c(c_re), spread_c(-c_im)], axis=1)

    def rows_of(p_re, p_im):
        k = p_re.shape[0]
        p_re, p_im = p_re.reshape(k, nq, qg * n), p_im.reshape(k, nq, qg * n)
        pad = ((0, 0), (0, 16 - k), (0, 0))
        return (jnp.pad(jnp.concatenate([p_re, p_re], axis=-1).transpose(1, 0, 2), pad),
                jnp.pad(jnp.concatenate([-p_im, p_im], axis=-1).transpose(1, 0, 2), pad))

    pbr, pbi = rows_of(pr[:blk][::-1], pi[:blk][::-1])
    pcr, pci = rows_of(pr[1:], pi[1:])
    apr, api = rows_of(*powers(float(blk) * (2.0 ** np.arange(nsteps))))
    return kc.astype(BF16), bbase.astype(BF16), cbase.astype(BF16), pbr, pbi, pcr, pci, apr, api


def _s5(p, lam_re, lam_im, log_step, b_re, b_im, c_re, c_im, d_skip):
    t = p.shape[0]
    rows = t // S5_BLK
    nsteps = max(1, int(math.ceil(math.log2(rows))))
    assert nsteps <= 16 and S5_OCT * S5_GROUP == 128
    nq, sw = S5_GROUPS // S5_OCT, 2 * S5_OCT_STATE
    tables = _s5_tables(lam_re, lam_im, log_step, b_re, b_im, c_re, c_im, nsteps)
    per_oct = lambda *shape: pl.BlockSpec((None,) + shape, lambda i: (i,) + (0,) * len(shape))
    return pl.pallas_call(
        functools.partial(_s5_kernel, rows=rows, nsteps=nsteps),
        out_shape=jax.ShapeDtypeStruct((t, S5_CH), F32),
        grid=(nq,),
        in_specs=[pl.BlockSpec((t, 128), lambda i: (0, C_SU // 128 + i)),
                  per_oct(128, S5_BLK * 128), per_oct(128, sw), per_oct(sw, 128)]
                 + [per_oct(16, sw)] * 6 + [per_oct(1, 128)],
        out_specs=pl.BlockSpec((t, 128), lambda i: (0, i)),
        scratch_shapes=[pltpu.VMEM((rows, S5_BLK * 128), F32)],
        compiler_params=_cp("arbitrary"),
        name="s5_scan",
    )(p, *tables, d_skip.reshape(nq, 1, 128))


def _ssd_kernel(z_ref, x_ref, b_ref, c_ref, sm_ref, cwx_ref, cwb_ref, cwc_ref, cbx_ref, cbb_ref, cbc_ref,
                dtb_ref, alog_ref, dsk_ref, nw_ref, o_ref, st_ref, tx_ref, tb_ref, tc_ref):
    @pl.when(pl.program_id(0) == 0)
    def _():
        st_ref[...] = jnp.zeros_like(st_ref)
        tx_ref[...] = jnp.zeros_like(tx_ref)
        tb_ref[...] = jnp.zeros_like(tb_ref)
        tc_ref[...] = jnp.zeros_like(tc_ref)

    c = SSD_CHUNK
    xin, bin_, cin = x_ref[...], b_ref[...], c_ref[...]
    xs = _silu(_causal_conv(xin, tx_ref[...], cwx_ref[...]) + cbx_ref[...])
    bs = _silu(_causal_conv(bin_, tb_ref[...], cwb_ref[...]) + cbb_ref[...])
    cs = _silu(_causal_conv(cin, tc_ref[...], cwc_ref[...]) + cbc_ref[...])
    tx_ref[...] = xin[c - 8:, :]
    tb_ref[...] = bin_[c - 8:, :]
    tc_ref[...] = cin[c - 8:, :]

    ii = lax.broadcasted_iota(jnp.int32, (c, c), 0)
    jj = lax.broadcasted_iota(jnp.int32, (c, c), 1)
    lower = ii >= jj
    dt_heads = _softplus(sm_ref[...] + dtb_ref[...])
    acum_heads = _chunk_cumsum(dt_heads * (-jnp.exp(alog_ref[...])), c)
    lane = lax.broadcasted_iota(jnp.int32, (c, 128), 1)
    lo = lane < SSD_HD
    npair = SSD_HEADS // 2
    per_group = npair // SSD_GROUPS
    pairs = range(npair)
    cols_of = [slice(pi * 128, (pi + 1) * 128) for pi in pairs]
    bgs = [bs[:, gi * SSD_STATE:(gi + 1) * SSD_STATE] for gi in range(SSD_GROUPS)]
    cgs = [cs[:, gi * SSD_STATE:(gi + 1) * SSD_STATE] for gi in range(SSD_GROUPS)]
    cbs = [_bdot_nt(cgs[gi], bgs[gi]) for gi in range(SSD_GROUPS)]
    col = lambda a, h: a[:, SSD_DT_LANE + h:SSD_DT_LANE + h + 1]
    acs = [jnp.where(lo, col(acum_heads, 2 * pi), col(acum_heads, 2 * pi + 1)) for pi in pairs]
    xds = [xs[:, cols_of[pi]] * jnp.where(lo, col(dt_heads, 2 * pi), col(dt_heads, 2 * pi + 1)) for pi in pairs]
    las = [_masked_decay(_pair_diff(col(acum_heads, 2 * pi), lane), lower) for pi in pairs]
    lbs = [_masked_decay(_pair_diff(col(acum_heads, 2 * pi + 1), lane), lower) for pi in pairs]
    sts = [st_ref[pi] for pi in pairs]
    y_off = [_bdot(cgs[pi // per_group], sts[pi]) for pi in pairs]
    y_a = [_bdot(cbs[pi // per_group] * las[pi], jnp.where(lo, xds[pi], 0.0)) for pi in pairs]
    y_b = [_bdot(cbs[pi // per_group] * lbs[pi], jnp.where(lo, 0.0, xds[pi])) for pi in pairs]
    alasts = [ac[c - 1:c, :] for ac in acs]
    st_in = [_bdot_tn(bgs[pi // per_group], xds[pi] * jnp.exp(alasts[pi] - acs[pi])) for pi in pairs]
    ys = []
    for pi in pairs:
        st_ref[pi] = sts[pi] * jnp.exp(alasts[pi]) + st_in[pi]
        y = y_a[pi] + y_b[pi] + y_off[pi] * jnp.exp(acs[pi])
        ys.append((y + dsk_ref[:, cols_of[pi]] * xs[:, cols_of[pi]]) * _silu(z_ref[:, cols_of[pi]]))
    for gi in range(SSD_GROUPS):
        members = range(gi * per_group, (gi + 1) * per_group)
        ss = sum(jnp.sum(ys[pi] * ys[pi], axis=-1, keepdims=True) for pi in members)
        scale = lax.rsqrt(ss / (SSD_INNER // SSD_GROUPS) + RMS_EPS)
        for pi in members:
            o_ref[:, cols_of[pi]] = (ys[pi] * scale * nw_ref[:, cols_of[pi]]).astype(o_ref.dtype)


def _ssd(p, conv_w, conv_b, dt_bias, a_log, d_skip, norm_w):
    t = p.shape[0]
    c = SSD_CHUNK
    nbc = SSD_GROUPS * SSD_STATE
    taps = lambda w: jnp.pad(w.T, ((0, 8 - CONV_WIDTH), (0, 0)))
    cwx, cwb, cwc = taps(conv_w[:SSD_INNER]), taps(conv_w[SSD_INNER:SSD_INNER + nbc]), taps(conv_w[SSD_INNER + nbc:])
    cbx = conv_b[:SSD_INNER].reshape(1, -1)
    cbb = conv_b[SSD_INNER:SSD_INNER + nbc].reshape(1, -1)
    cbc = conv_b[SSD_INNER + nbc:].reshape(1, -1)
    on_dt_lanes = lambda a: jnp.pad(a, (SSD_DT_LANE, 128 - SSD_DT_LANE - SSD_HEADS)).reshape(1, 128)
    full = lambda n: pl.BlockSpec((1, n), lambda i: (0, 0))
    return pl.pallas_call(
        _ssd_kernel,
        out_shape=jax.ShapeDtypeStruct((t, SSD_INNER), BF16),
        grid=(t // c,),
        in_specs=[pl.BlockSpec((c, SSD_INNER), lambda i: (i, C_MZ // SSD_INNER)),
                  pl.BlockSpec((c, SSD_INNER), lambda i: (i, C_MX // SSD_INNER)),
                  pl.BlockSpec((c, nbc), lambda i: (i, C_MB // nbc)),
                  pl.BlockSpec((c, nbc), lambda i: (i, C_MC // nbc)),
                  pl.BlockSpec((c, 128), lambda i: (i, C_SMALL // 128)),
                  pl.BlockSpec((8, SSD_INNER), lambda i: (0, 0)),
                  pl.BlockSpec((8, nbc), lambda i: (0, 0)),
                  pl.BlockSpec((8, nbc), lambda i: (0, 0)),
                  full(SSD_INNER), full(nbc), full(nbc),
                  full(128), full(128), full(SSD_INNER), full(SSD_INNER)],
        out_specs=pl.BlockSpec((c, SSD_INNER), lambda i: (i, 0)),
        scratch_shapes=[pltpu.VMEM((SSD_HEADS // 2, SSD_STATE, 128), F32),
                        pltpu.VMEM((8, SSD_INNER), F32), pltpu.VMEM((8, nbc), F32), pltpu.VMEM((8, nbc), F32)],
        compiler_params=_cp("arbitrary"),
        name="ssd",
    )(p, p, p, p, p, cwx, cwb, cwc, cbx, cbb, cbc, on_dt_lanes(dt_bias), on_dt_lanes(a_log),
      jnp.repeat(d_skip, SSD_HD).reshape(1, SSD_INNER), norm_w.reshape(1, SSD_INNER))


def kernel(x, positions, w_in, gdn_conv_w, gdn_a_log, gdn_dt_bias, gdn_norm_w, s5_lam_re, s5_lam_im, s5_log_step, s5_b_re, s5_b_im, s5_c_re, s5_c_im, s5_d, s5_glu_w, s5_glu_b, ssd_conv_w, ssd_conv_b, ssd_dt_bias, ssd_a_log, ssd_d, ssd_norm_w, w_branch_ret, w_branch_gdn, w_branch_s5, w_branch_ssd, w_out, ln1_g, ln1_b, w_up, w_down, ln2_g, ln2_b):
    bsz, t, d = x.shape
    assert bsz == 1 and d == D_MODEL
    xf = x.reshape(t, d)
    xb = xf.astype(BF16)
    cos, sin = _rope_tables(positions.reshape(t), min(t, 1024))
    for l in range(DEPTH):
        p = _proj_in(xb, w_in[l], 1024)
        y_ret = _retention(p, cos, sin, 512)
        y_gdn = _gdn(p, gdn_conv_w[l], gdn_a_log[l], gdn_dt_bias[l], gdn_norm_w[l], 512)
        v_s5 = _s5(p, s5_lam_re[l], s5_lam_im[l], s5_log_step[l], s5_b_re[l], s5_b_im[l],
                   s5_c_re[l], s5_c_im[l], s5_d[l])
        y_s5 = _glu(v_s5, s5_glu_w[l].astype(BF16), s5_glu_b[l], 512)
        y_ssd = _ssd(p, ssd_conv_w[l], ssd_conv_b[l], ssd_dt_bias[l], ssd_a_log[l], ssd_d[l], ssd_norm_w[l])
        ws = [w.astype(BF16) for w in (w_branch_ret[l], w_branch_gdn[l], w_branch_s5[l], w_branch_ssd[l])]
        merged = _merge(p, [y_ret, y_gdn, y_s5, y_ssd], ws, 1024, 512)
        xf, xb = _matmul_res_ln(merged, w_out[l].astype(BF16), xf, ln1_g[l], ln1_b[l], 512, D_MODEL)
        hid = _up_proj(xb, w_up[l], 1024, 1024)
        xf, xb = _matmul_res_ln(hid, w_down[l].astype(BF16), xf, ln2_g[l], ln2_b[l], 512, 2048)
    return xf.reshape(bsz, t, d)
```

```python
import functools
import math

import numpy as np
import jax
import jax.numpy as jnp
from jax import lax
from jax.experimental import pallas as pl
from jax.experimental.pallas import tpu as pltpu

F32 = jnp.float32
BF16 = jnp.bfloat16
HI = lax.Precision.HIGHEST

D_MODEL = 2048
DEPTH = 2
RET_HEADS, RET_QK, RET_V, RET_CHUNK = 4, 128, 256, 128
ROPE_BASE = 10000.0
GDN_HEADS, GDN_DIM, GDN_CHUNK = 8, 128, 64
CONV_WIDTH = 4
S5_CH, S5_GROUP, S5_STATE = 1024, 16, 64
S5_GROUPS = S5_CH // S5_GROUP
S5_BLK = 16
S5_OCT = 8
S5_OCT_STATE = S5_OCT * S5_STATE
SSD_HEADS, SSD_HD, SSD_GROUPS, SSD_STATE, SSD_CHUNK = 16, 64, 2, 128, 128
SSD_INNER = SSD_HEADS * SSD_HD
D_FF = 4 * D_MODEL
ALPHA = (2 * DEPTH) ** 0.25
LN_EPS = 1e-5
RMS_EPS = 1e-6

C_0 = 1024
C_RQ, C_RK, C_RV, C_RG = C_0, C_0 + 512, C_0 + 1024, C_0 + 2048
C_DQ, C_DK, C_DV, C_DZ = C_0 + 3072, C_0 + 4096, C_0 + 5120, C_0 + 6144
C_SU, C_MZ, C_MX, C_MB, C_MC = C_0 + 7168, C_0 + 8192, C_0 + 9216, C_0 + 10240, C_0 + 10496
C_GATE = C_0 + 10752
C_SMALL = C_0 + 18944
SSD_DT_LANE = 16
NP = C_0 + 19456

VMEM_LIMIT = 52 * 1024 * 1024


def _cp(*sem):
    return pltpu.CompilerParams(dimension_semantics=sem, vmem_limit_bytes=VMEM_LIMIT)


def _dot(a, b, prec=None):
    return lax.dot_general(a, b, (((1,), (0,)), ((), ())), precision=prec, preferred_element_type=F32)


def _dot_nt(a, b, prec=None):
    return lax.dot_general(a, b, (((1,), (1,)), ((), ())), precision=prec, preferred_element_type=F32)


def _dot_tn(a, b, prec=None):
    return lax.dot_general(a, b, (((0,), (0,)), ((), ())), precision=prec, preferred_element_type=F32)


def _bdot(a, b):
    return _dot(a.astype(BF16), b.astype(BF16))


def _bdot_nt(a, b):
    return _dot_nt(a.astype(BF16), b.astype(BF16))


def _bdot_tn(a, b):
    return _dot_tn(a.astype(BF16), b.astype(BF16))


def _silu(x):
    return x * (1.0 / (1.0 + jnp.exp(-x)))


def _sigmoid(x):
    return 1.0 / (1.0 + jnp.exp(-x))


def _softplus(x):
    return jnp.maximum(x, 0.0) + jnp.log1p(jnp.exp(-jnp.abs(x)))


def _masked_decay(diff, lower):
    return jnp.where(lower, jnp.exp(jnp.where(lower, diff, 0.0)), 0.0)


def _pair_diff(col, lane):
    hi = col.astype(BF16).astype(F32)
    r1 = col - hi
    mid = r1.astype(BF16).astype(F32)
    lo = (r1 - mid).astype(BF16).astype(F32)
    a = jnp.where(lane == 0, hi, jnp.where(lane == 1, mid, jnp.where(lane == 2, lo, jnp.where(lane < 6, 1.0, 0.0))))
    b = jnp.where(lane < 3, 1.0, jnp.where(lane == 3, -hi, jnp.where(lane == 4, -mid, jnp.where(lane == 5, -lo, 0.0))))
    return _bdot_nt(a, b)


def _chunk_cumsum(x, chunk):
    row = lax.broadcasted_iota(jnp.int32, x.shape, 0) & (chunk - 1)
    sh = 1
    while sh < chunk:
        x = x + jnp.where(row >= sh, pltpu.roll(x, sh, 0), 0.0)
        sh *= 2
    return x


def _shifted(x, tail, s, row8):
    xs = pltpu.roll(x, s, 0)
    ts = pltpu.roll(tail, s, 0)
    first = jnp.where(row8 < s, ts, xs[:8])
    return jnp.concatenate([first, xs[8:]], axis=0)


def _causal_conv(x, tail, w):
    row8 = lax.broadcasted_iota(jnp.int32, (8, x.shape[1]), 0)
    acc = x * w[3:4, :]
    for s in (1, 2, 3):
        acc = acc + _shifted(x, tail, s, row8) * w[3 - s:4 - s, :]
    return acc


def _stage_first_tile(w0_ref, wb_ref, chunk):
    for r in range(0, w0_ref.shape[0], chunk):
        wb_ref[0, r:r + chunk, :] = w0_ref[r:r + chunk, :].astype(BF16)


def _up_kernel(a_ref, w0_ref, wn_ref, o_ref, wb_ref, *, chunk):
    j, i = pl.program_id(0), pl.program_id(1)

    @pl.when((j == 0) & (i == 0))
    def _():
        _stage_first_tile(w0_ref, wb_ref, chunk)

    rows = pl.ds(pl.multiple_of(i * chunk, chunk), chunk)
    wb_ref[(j + 1) % 2, rows, :] = wn_ref[rows, :].astype(BF16)
    acc = jnp.dot(a_ref[...], wb_ref[j % 2], preferred_element_type=F32)
    o_ref[...] = jnp.square(jnp.maximum(acc, 0.0)).astype(o_ref.dtype)


def _up_proj(a, w, layer, tm, tn):
    m, k = a.shape
    n = w.shape[2]
    nj, ni = n // tn, m // tm
    chunk = k // ni
    assert chunk * ni == k and chunk % 16 == 0
    return pl.pallas_call(
        functools.partial(_up_kernel, chunk=chunk),
        out_shape=jax.ShapeDtypeStruct((m, n), BF16),
        grid=(nj, ni),
        in_specs=[pl.BlockSpec((tm, k), lambda j, i: (i, 0)),
                  pl.BlockSpec((None, k, tn), lambda j, i: (layer, 0, 0)),
                  pl.BlockSpec((None, k, tn), lambda j, i: (layer, 0, jnp.minimum(j + 1, nj - 1)))],
        out_specs=pl.BlockSpec((tm, tn), lambda j, i: (i, j)),
        scratch_shapes=[pltpu.VMEM((2, k, tn), BF16)],
        compiler_params=_cp("arbitrary", "arbitrary"),
        name="up_proj",
    )(a, w, w)


PROJ_TN = 512
PROJ_T1, PROJ_T2, PROJ_TAIL = (C_SU - C_0) // PROJ_TN, (C_GATE - C_0) // PROJ_TN, (C_SMALL - C_0) // PROJ_TN
D_IN_PROJ = 18976
SRC_DA, SRC_MDT = 7168, 10768


def _proj_kernel(a_ref, wa0_ref, wn0_ref, wa1_ref, wn1_ref, o_ref, wb_ref, *, chunk):
    j, i = pl.program_id(0), pl.program_id(1)
    body = PROJ_TN - 128

    @pl.when((j == 0) & (i == 0))
    def _():
        wb_ref[1] = jnp.zeros(wb_ref.shape[1:], BF16)

    rows = pl.ds(pl.multiple_of(i * chunk, chunk), chunk)
    lane = lax.broadcasted_iota(jnp.int32, (chunk, 128), 1)
    slot = j % 2
    for half, (wa_ref, wn_ref) in enumerate(((wa0_ref, wn0_ref), (wa1_ref, wn1_ref))):
        t = 2 * j + half
        c0 = half * PROJ_TN
        sh = jnp.where(t < PROJ_T1, 0, jnp.where(t < PROJ_T2, 16, 32))
        ra = pltpu.roll(wa_ref[rows, :], jnp.where(sh == 0, 0, PROJ_TN - sh), 1)
        rn = pltpu.roll(wn_ref[rows, :], jnp.where(sh == 0, 0, 128 - sh), 1)
        wb_ref[slot, rows, c0:c0 + body] = ra[:, :body].astype(BF16)
        wb_ref[slot, rows, c0 + body:c0 + PROJ_TN] = jnp.where(lane < 128 - sh, ra[:, body:], rn).astype(BF16)

    @pl.when(2 * j + 1 == PROJ_TAIL)
    def _():
        first = jnp.where(lane < 16, wa1_ref[rows, :128], jnp.where(lane < 32, wn1_ref[rows, :], 0.0))
        wb_ref[slot, rows, PROJ_TN:PROJ_TN + 128] = first.astype(BF16)
        wb_ref[slot, rows, PROJ_TN + 128:] = jnp.zeros((chunk, body), BF16)

    o_ref[...] = jnp.dot(a_ref[...], wb_ref[(j + 1) % 2], preferred_element_type=F32)


def _proj_in(a, w, layer, tm):
    m, k = a.shape
    assert w.shape[2] == D_IN_PROJ and NP - C_0 == (PROJ_TAIL + 1) * PROJ_TN and C_0 == 2 * PROJ_TN
    assert SRC_DA % PROJ_TN == 0 and SRC_MDT % 128 == 16 and PROJ_TAIL % 2 == 1
    ni = m // tm
    chunk = k // ni
    assert chunk * ni == k and chunk % 16 == 0
    per = PROJ_TN // 128
    npair = (NP - C_0) // (2 * PROJ_TN)

    def tile(j, half):
        return 2 * jnp.minimum(j, npair - 1) + half

    def wa_idx(half):
        return lambda j, i: (layer, 0, jnp.where(tile(j, half) == PROJ_TAIL, SRC_DA // PROJ_TN, tile(j, half)))

    def wn_idx(half):
        return lambda j, i: (layer, 0, jnp.where(tile(j, half) == PROJ_TAIL, SRC_MDT // 128, (tile(j, half) + 1) * per))

    w_specs = []
    for half in range(2):
        w_specs += [pl.BlockSpec((None, k, PROJ_TN), wa_idx(half)), pl.BlockSpec((None, k, 128), wn_idx(half))]
    return pl.pallas_call(
        functools.partial(_proj_kernel, chunk=chunk),
        out_shape=jax.ShapeDtypeStruct((m, NP), F32),
        grid=(npair + 1, ni),
        in_specs=[pl.BlockSpec((tm, k), lambda j, i: (i, 0))] + w_specs,
        out_specs=pl.BlockSpec((tm, 2 * PROJ_TN), lambda j, i: (i, j)),
        scratch_shapes=[pltpu.VMEM((2, k, 2 * PROJ_TN), BF16)],
        compiler_params=_cp("arbitrary", "arbitrary"),
        name="proj_in",
    )(a, w, w, w, w)


def _mm_ln_kernel(a_ref, b_ref, res_ref, g_ref, beta_ref, of_ref, ob_ref, acc_ref, *, nk):
    k = pl.program_id(1)

    @pl.when(k == 0)
    def _():
        acc_ref[...] = jnp.zeros_like(acc_ref)

    acc_ref[...] += jnp.dot(a_ref[...], b_ref[...], preferred_element_type=F32)

    @pl.when(k == nk - 1)
    def _():
        y = ALPHA * res_ref[...] + acc_ref[...]
        mu = jnp.mean(y, axis=-1, keepdims=True)
        d = y - mu
        var = jnp.mean(d * d, axis=-1, keepdims=True)
        out = d * lax.rsqrt(var + LN_EPS) * g_ref[...] + beta_ref[...]
        of_ref[...] = out
        ob_ref[...] = out.astype(BF16)


def _matmul_res_ln(a, b, layer, res, g, beta, tm, tk):
    m, k = a.shape
    n = b.shape[2]
    nk = k // tk
    return pl.pallas_call(
        functools.partial(_mm_ln_kernel, nk=nk),
        out_shape=(jax.ShapeDtypeStruct((m, n), F32), jax.ShapeDtypeStruct((m, n), BF16)),
        grid=(m // tm, nk),
        in_specs=[pl.BlockSpec((tm, tk), lambda i, kk: (i, kk)),
                  pl.BlockSpec((None, tk, n), lambda i, kk: (layer, kk, 0)),
                  pl.BlockSpec((tm, n), lambda i, kk: (i, 0)),
                  pl.BlockSpec((1, n), lambda i, kk: (0, 0)),
                  pl.BlockSpec((1, n), lambda i, kk: (0, 0))],
        out_specs=(pl.BlockSpec((tm, n), lambda i, kk: (i, 0)),
                   pl.BlockSpec((tm, n), lambda i, kk: (i, 0))),
        scratch_shapes=[pltpu.VMEM((tm, n), F32)],
        compiler_params=_cp("arbitrary", "arbitrary"),
        name="matmul_res_ln",
    )(a, b, res, g.reshape(1, n), beta.reshape(1, n))


def _merge_kernel(y0, y1, y2, y3, w0, w1, w2, w3, g0, g1, g2, g3, o_ref):
    acc = None
    for y_ref, w_ref, g_ref in ((y0, w0, g0), (y1, w1, g1), (y2, w2, g2), (y3, w3, g3)):
        t = _sigmoid(g_ref[...]) * jnp.dot(y_ref[...], w_ref[...], preferred_element_type=F32)
        acc = t if acc is None else acc + t
    o_ref[...] = acc.astype(o_ref.dtype)


def _merge(p, ys, ws, layer, tm, tn):
    t = p.shape[0]
    kb = ys[0].shape[1]
    gate_blk = C_GATE // tn
    per_branch = D_MODEL // tn
    y_specs = [pl.BlockSpec((tm, kb), lambda i, j: (i, 0)) for _ in range(4)]
    w_specs = [pl.BlockSpec((None, kb, tn), lambda i, j: (layer, 0, j)) for _ in range(4)]
    g_specs = [pl.BlockSpec((tm, tn), functools.partial(lambda i, j, b: (i, gate_blk + b * per_branch + j), b=b))
               for b in range(4)]
    return pl.pallas_call(
        _merge_kernel,
        out_shape=jax.ShapeDtypeStruct((t, D_MODEL), BF16),
        grid=(t // tm, D_MODEL // tn),
        in_specs=y_specs + w_specs + g_specs,
        out_specs=pl.BlockSpec((tm, tn), lambda i, j: (i, j)),
        compiler_params=_cp("arbitrary", "arbitrary"),
        name="merge",
    )(*ys, *ws, p, p, p, p)


def _glu_kernel(v_ref, w_ref, b_ref, o_ref):
    v = v_ref[...]
    z = jnp.dot(v.astype(BF16), w_ref[...], preferred_element_type=F32) + b_ref[...]
    o_ref[...] = (v * _sigmoid(z)).astype(o_ref.dtype)


def _glu(v, w, layer, b, tm):
    t, n = v.shape
    return pl.pallas_call(
        _glu_kernel,
        out_shape=jax.ShapeDtypeStruct((t, n), BF16),
        grid=(t // tm,),
        in_specs=[pl.BlockSpec((tm, n), lambda i: (i, 0)),
                  pl.BlockSpec((None, n, n), lambda i: (layer, 0, 0)),
                  pl.BlockSpec((1, n), lambda i: (0, 0))],
        out_specs=pl.BlockSpec((tm, n), lambda i: (i, 0)),
        compiler_params=_cp("arbitrary"),
        name="s5_glu",
    )(v, w, b.reshape(1, n))


def _rope_kernel(pos_ref, inv_ref, cos_ref, sin_ref):
    ang = pos_ref[...].astype(F32) * inv_ref[...]
    lane = lax.broadcasted_iota(jnp.int32, ang.shape, 1)
    s = jnp.sin(ang)
    cos_ref[...] = jnp.cos(ang)
    sin_ref[...] = jnp.where(lane < RET_QK // 2, -s, s)


def _rope_tables(positions, tb):
    t = positions.shape[0]
    half = RET_QK // 2
    inv = (ROPE_BASE ** (-np.arange(half, dtype=np.float64) / half)).astype(np.float32)
    inv = jnp.asarray(np.concatenate([inv, inv])[None, :])
    return pl.pallas_call(
        _rope_kernel,
        out_shape=(jax.ShapeDtypeStruct((t, RET_QK), F32), jax.ShapeDtypeStruct((t, RET_QK), F32)),
        grid=(t // tb,),
        in_specs=[pl.BlockSpec((tb, 1), lambda i: (i, 0)),
                  pl.BlockSpec((1, RET_QK), lambda i: (0, 0))],
        out_specs=(pl.BlockSpec((tb, RET_QK), lambda i: (i, 0)),
                   pl.BlockSpec((tb, RET_QK), lambda i: (i, 0))),
        compiler_params=_cp("arbitrary"),
        name="rope_tables",
    )(positions.reshape(t, 1), inv)


def _ret_kernel(q_ref, k_ref, v_ref, g_ref, cos_ref, sin_ref, dmat_ref, qdec_ref, kdec_ref, gam_ref,
                o_ref, r_ref, *, nchunk):
    @pl.when(pl.program_id(1) == 0)
    def _():
        r_ref[...] = jnp.zeros_like(r_ref)

    c = RET_CHUNK
    dmat = dmat_ref[...]
    qdec = qdec_ref[...]
    kdec = kdec_ref[...]
    gam = gam_ref[0:1, :]
    for ci in range(nchunk):
        rows = slice(ci * c, (ci + 1) * c)
        cos = cos_ref[rows, :]
        sin = sin_ref[rows, :]
        q = q_ref[rows, :]
        k = k_ref[rows, :]
        v = v_ref[rows, :]
        qr = q * cos + pltpu.roll(q, RET_QK // 2, 1) * sin
        kr = (k * cos + pltpu.roll(k, RET_QK // 2, 1) * sin) * (RET_QK ** -0.5)
        r = r_ref[...]
        scores = _bdot_nt(qr, kr) * dmat
        y = _bdot(scores, v) + _bdot(qr * qdec, r)
        r_ref[...] = r * gam + _bdot_tn(kr * kdec, v)
        mu = jnp.mean(y, axis=-1, keepdims=True)
        d = y - mu
        var = jnp.mean(d * d, axis=-1, keepdims=True)
        yn = d * lax.rsqrt(var + LN_EPS)
        o_ref[rows, :] = (_silu(g_ref[rows, :]) * yn).astype(o_ref.dtype)


def _ret_constants():
    c = RET_CHUNK
    h = np.arange(RET_HEADS, dtype=np.float64)
    log_g = np.log(1.0 - 2.0 ** (-5.0 - h))
    idx = np.arange(c, dtype=np.float64)
    diff = idx[:, None] - idx[None, :]
    dmat = np.where(diff >= 0, np.exp(np.where(diff >= 0, diff, 0.0)[None] * log_g[:, None, None]), 0.0)
    qdec = np.exp((idx + 1.0)[None, :] * log_g[:, None])
    kdec = np.exp((c - 1.0 - idx)[None, :] * log_g[:, None])
    gam = np.exp(c * log_g)
    qdec = np.broadcast_to(qdec[:, :, None], (RET_HEADS, c, RET_QK))
    kdec = np.broadcast_to(kdec[:, :, None], (RET_HEADS, c, RET_QK))
    gam = np.broadcast_to(gam[:, None, None], (RET_HEADS, 8, RET_V))
    f = lambda a: jnp.asarray(np.ascontiguousarray(a), dtype=F32)
    return f(dmat), f(qdec), f(kdec), f(gam)


def _retention(p, cos, sin, tb):
    t = p.shape[0]
    dmat, qdec, kdec, gam = _ret_constants()
    c = RET_CHUNK
    return pl.pallas_call(
        functools.partial(_ret_kernel, nchunk=tb // c),
        out_shape=jax.ShapeDtypeStruct((t, RET_HEADS * RET_V), BF16),
        grid=(RET_HEADS, t // tb),
        in_specs=[pl.BlockSpec((tb, RET_QK), lambda h, i: (i, C_RQ // RET_QK + h)),
                  pl.BlockSpec((tb, RET_QK), lambda h, i: (i, C_RK // RET_QK + h)),
                  pl.BlockSpec((tb, RET_V), lambda h, i: (i, C_RV // RET_V + h)),
                  pl.BlockSpec((tb, RET_V), lambda h, i: (i, C_RG // RET_V + h)),
                  pl.BlockSpec((tb, RET_QK), lambda h, i: (i, 0)),
                  pl.BlockSpec((tb, RET_QK), lambda h, i: (i, 0)),
                  pl.BlockSpec((None, c, c), lambda h, i: (h, 0, 0)),
                  pl.BlockSpec((None, c, RET_QK), lambda h, i: (h, 0, 0)),
                  pl.BlockSpec((None, c, RET_QK), lambda h, i: (h, 0, 0)),
                  pl.BlockSpec((None, 8, RET_V), lambda h, i: (h, 0, 0))],
        out_specs=pl.BlockSpec((tb, RET_V), lambda h, i: (i, h)),
        scratch_shapes=[pltpu.VMEM((RET_QK, RET_V), F32)],
        compiler_params=_cp("arbitrary", "arbitrary"),
        name="retention",
    )(p, p, p, p, cos, sin, dmat, qdec, kdec, gam)


GDN_SUB = 16


def _gdn_kernel(q_ref, k_ref, v_ref, z_ref, sm_ref, cw_ref, hp_ref, nw_ref, o_ref,
                s_ref, tq_ref, tk_ref, tv_ref, *, nchunk):
    h = pl.program_id(0)

    @pl.when(pl.program_id(1) == 0)
    def _():
        s_ref[...] = jnp.zeros_like(s_ref)
        tq_ref[...] = jnp.zeros_like(tq_ref)
        tk_ref[...] = jnp.zeros_like(tk_ref)
        tv_ref[...] = jnp.zeros_like(tv_ref)

    c = GDN_CHUNK
    tb = nchunk * c
    xq, xk, xv = q_ref[...], k_ref[...], v_ref[...]
    q = _silu(_causal_conv(xq, tq_ref[...], cw_ref[0]))
    k = _silu(_causal_conv(xk, tk_ref[...], cw_ref[1]))
    v = _silu(_causal_conv(xv, tv_ref[...], cw_ref[2]))
    tq_ref[...] = xq[tb - 8:, :]
    tk_ref[...] = xk[tb - 8:, :]
    tv_ref[...] = xv[tb - 8:, :]
    q = q * lax.rsqrt(jnp.sum(q * q, axis=-1, keepdims=True) + RMS_EPS) * (GDN_DIM ** -0.5)
    k = k * lax.rsqrt(jnp.sum(k * k, axis=-1, keepdims=True) + RMS_EPS)

    sm = sm_ref[...]
    lane_b = lax.broadcasted_iota(jnp.int32, sm.shape, 1)
    g_heads = -jnp.exp(hp_ref[0:1, :]) * _softplus(sm + hp_ref[1:2, :])
    cum_heads = _chunk_cumsum(g_heads, c)
    gcum = jnp.sum(jnp.where(lane_b == h, cum_heads, 0.0), axis=-1, keepdims=True)
    beta = _sigmoid(jnp.sum(jnp.where(lane_b == h + GDN_HEADS, sm, 0.0), axis=-1, keepdims=True))
    eg = jnp.exp(gcum)

    ii = lax.broadcasted_iota(jnp.int32, (c, c), 0)
    jj = lax.broadcasted_iota(jnp.int32, (c, c), 1)
    lower = ii >= jj
    strict = ii > jj
    same_sub = jnp.right_shift(ii, 4) == jnp.right_shift(jj, 4)
    eye = (ii == jj).astype(F32)
    lane = lax.broadcasted_iota(jnp.int32, (c, GDN_DIM), 1)
    nw = nw_ref[...]

    chunks = range(nchunk)
    rows_of = [slice(ci * c, (ci + 1) * c) for ci in chunks]
    qs, ks, vs = [q[r] for r in rows_of], [k[r] for r in rows_of], [v[r] for r in rows_of]
    bs, gs, egs = [beta[r] for r in rows_of], [gcum[r] for r in rows_of], [eg[r] for r in rows_of]
    gammas = [_masked_decay(_pair_diff(gs[ci], lane), lower) for ci in chunks]
    ms = [bs[ci] * _bdot_nt(ks[ci], ks[ci]) * gammas[ci] for ci in chunks]
    qks = [_bdot_nt(qs[ci], ks[ci]) * gammas[ci] for ci in chunks]
    pws = [jnp.where(strict & same_sub, -m, 0.0) for m in ms]
    loffs = [jnp.where(strict & jnp.logical_not(same_sub), m, 0.0) for m in ms]
    dinvs = [eye + pw for pw in pws]
    for _ in range(3):
        pws = [_bdot(pw, pw) for pw in pws]
        dinvs = [dinvs[ci] + _bdot(dinvs[ci], pws[ci]) for ci in chunks]
    es = [-_bdot(dinvs[ci], loffs[ci]) for ci in chunks]
    ys = [_bdot(dinvs[ci], jnp.concatenate([vs[ci] * bs[ci], ks[ci] * (bs[ci] * egs[ci])], axis=1)) for ci in chunks]
    e2s = [_bdot(e, e) for e in es]
    ys = [ys[ci] + _bdot(es[ci], ys[ci]) for ci in chunks]
    ys = [ys[ci] + _bdot(e2s[ci], ys[ci]) for ci in chunks]
    glasts = [g[c - 1:c, :] for g in gs]
    kty = [_bdot_tn(ks[ci] * jnp.exp(glasts[ci] - gs[ci]), ys[ci]) for ci in chunks]
    qky = [_bdot(qks[ci], ys[ci]) for ci in chunks]
    s = s_ref[...]
    for ci in chunks:
        sb = s.astype(BF16)
        s_next = s * jnp.exp(glasts[ci]) + kty[ci][:, :GDN_DIM] - _dot(kty[ci][:, GDN_DIM:].astype(BF16), sb)
        o = _dot((qs[ci] * egs[ci] - qky[ci][:, GDN_DIM:]).astype(BF16), sb) + qky[ci][:, :GDN_DIM]
        s = s_next
        o = o * lax.rsqrt(jnp.mean(o * o, axis=-1, keepdims=True) + RMS_EPS) * nw
        o_ref[rows_of[ci], :] = (o * _silu(z_ref[rows_of[ci], :])).astype(o_ref.dtype)
    s_ref[...] = s


def _gdn(p, conv_w, a_log, dt_bias, norm_w, tb):
    t = p.shape[0]
    hd = GDN_DIM
    assert GDN_CHUNK == 4 * GDN_SUB
    cw = conv_w.reshape(3, GDN_HEADS, hd, CONV_WIDTH).transpose(0, 1, 3, 2)
    cw = jnp.pad(cw, ((0, 0), (0, 0), (0, 8 - CONV_WIDTH), (0, 0)))
    hp = jnp.pad(jnp.stack([a_log, dt_bias], axis=0), ((0, 6), (0, 128 - GDN_HEADS)))
    blk = lambda off: (lambda h, i: (i, off // hd + h))
    return pl.pallas_call(
        functools.partial(_gdn_kernel, nchunk=tb // GDN_CHUNK),
        out_shape=jax.ShapeDtypeStruct((t, GDN_HEADS * hd), BF16),
        grid=(GDN_HEADS, t // tb),
        in_specs=[pl.BlockSpec((tb, hd), blk(C_DQ)),
                  pl.BlockSpec((tb, hd), blk(C_DK)),
                  pl.BlockSpec((tb, hd), blk(C_DV)),
                  pl.BlockSpec((tb, hd), blk(C_DZ)),
                  pl.BlockSpec((tb, 128), lambda h, i: (i, C_SMALL // 128)),
                  pl.BlockSpec((3, None, 8, hd), lambda h, i: (0, h, 0, 0)),
                  pl.BlockSpec((8, 128), lambda h, i: (0, 0)),
                  pl.BlockSpec((1, hd), lambda h, i: (0, 0))],
        out_specs=pl.BlockSpec((tb, hd), lambda h, i: (i, h)),
        scratch_shapes=[pltpu.VMEM((hd, hd), F32), pltpu.VMEM((8, hd), F32),
                        pltpu.VMEM((8, hd), F32), pltpu.VMEM((8, hd), F32)],
        compiler_params=_cp("arbitrary", "arbitrary"),
        name="gated_deltanet",
    )(p, p, p, p, p, cw, hp, norm_w.reshape(1, hd))


def _gelu_tanh(x):
    return 0.5 * x * (1.0 + jnp.tanh(math.sqrt(2.0 / math.pi) * (x + 0.044715 * (x * x * x))))


def _crot(x, p_re, p_im):
    return p_re * x + p_im * pltpu.roll(x, S5_OCT_STATE, 1)


def _s5_kernel(u_ref, kc_ref, bb_ref, cb_ref, pbr_ref, pbi_ref, pcr_ref, pci_ref, apr_ref, api_ref, dsk_ref,
               o_ref, y_ref, *, rows, nsteps):
    blk = S5_BLK
    us = [u_ref[pl.ds(s, rows, stride=blk), :] for s in range(blk)]
    ubs = [u.astype(BF16) for u in us]
    for s in range(blk):
        w = _dot(ubs[s], kc_ref[:, :(blk - s) * 128])
        if s == 0:
            y_ref[...] = w
        else:
            y_ref[:, s * 128:] += w
    x = None
    for s in range(blk):
        term = _crot(_dot(ubs[s], bb_ref[...]), pbr_ref[s:s + 1, :], pbi_ref[s:s + 1, :])
        x = term if x is None else x + term
    row = lax.broadcasted_iota(jnp.int32, x.shape, 0)
    for kstep in range(nsteps):
        sh = 1 << kstep
        xs = jnp.where(row >= sh, pltpu.roll(x, sh, 0), 0.0)
        x = x + _crot(xs, apr_ref[kstep:kstep + 1, :], api_ref[kstep:kstep + 1, :])
    xprev = jnp.where(row >= 1, pltpu.roll(x, 1, 0), 0.0)
    for t in range(blk):
        carried = _crot(xprev, pcr_ref[t:t + 1, :], pci_ref[t:t + 1, :]).astype(BF16)
        y = y_ref[:, t * 128:(t + 1) * 128] + _dot(carried, cb_ref[...])
        o_ref[pl.ds(t, rows, stride=blk), :] = _gelu_tanh(y + dsk_ref[...] * us[t])


def _s5_tables(lam_re, lam_im, log_step, b_re, b_im, c_re, c_im, nsteps):
    g, n = lam_re.shape
    blk, qg, cg = S5_BLK, S5_OCT, S5_GROUP
    nq = g // qg
    step = jnp.exp(log_step)[:, None]
    zr, zi = lam_re * step, lam_im * step

    def powers(m):
        m = jnp.asarray(m, F32)[:, None, None]
        mag = jnp.exp(zr[None] * m)
        return mag * jnp.cos(zi[None] * m), mag * jnp.sin(zi[None] * m)

    ab_re, ab_im = jnp.exp(zr) * jnp.cos(zi), jnp.exp(zr) * jnp.sin(zi)
    den = lam_re * lam_re + lam_im * lam_im
    f_re = ((ab_re - 1.0) * lam_re + ab_im * lam_im) / den
    f_im = (ab_im * lam_re - (ab_re - 1.0) * lam_im) / den
    bb_re = f_re[..., None] * b_re - f_im[..., None] * b_im
    bb_im = f_re[..., None] * b_im + f_im[..., None] * b_re
    pr, pi = powers(np.arange(blk + 1))
    cpr = c_re[None] * pr[:, :, None, :] - c_im[None] * pi[:, :, None, :]
    cpi = c_re[None] * pi[:, :, None, :] + c_im[None] * pr[:, :, None, :]
    bt_re, bt_im = bb_re.transpose(0, 2, 1)[None, :, None], bb_im.transpose(0, 2, 1)[None, :, None]
    kern = jnp.sum(cpr[:blk, :, :, None, :] * bt_re - cpi[:blk, :, :, None, :] * bt_im, axis=-1)
    eye = jnp.eye(qg, dtype=F32)
    kc = jnp.einsum('tqgcd,gh->qgdthc', kern.reshape(blk, nq, qg, cg, cg), eye).reshape(nq, qg * cg, blk * qg * cg)
    spread_b = lambda b: jnp.einsum('qgnd,gh->qgdhn', b.reshape(nq, qg, n, cg), eye).reshape(nq, qg * cg, qg * n)
    bbase = jnp.concatenate([spread_b(bb_re), spread_b(bb_im)], axis=-1)
    spread_c = lambda c: jnp.einsum('qgcn,gh->qgnhc', c.reshape(nq, qg, cg, n), eye).reshape(nq, qg * n, qg * cg)
    cbase = jnp.concatenate([spread_c(c_re), spread_c(-c_im)], axis=1)

    def rows_of(p_re, p_im):
        k = p_re.shape[0]
        p_re, p_im = p_re.reshape(k, nq, qg * n), p_im.reshape(k, nq, qg * n)
        pad = ((0, 0), (0, 16 - k), (0, 0))
        return (jnp.pad(jnp.concatenate([p_re, p_re], axis=-1).transpose(1, 0, 2), pad),
                jnp.pad(jnp.concatenate([-p_im, p_im], axis=-1).transpose(1, 0, 2), pad))

    pbr, pbi = rows_of(pr[:blk][::-1], pi[:blk][::-1])
    pcr, pci = rows_of(pr[1:], pi[1:])
    apr, api = rows_of(*powers(float(blk) * (2.0 ** np.arange(nsteps))))
    return kc.astype(BF16), bbase.astype(BF16), cbase.astype(BF16), pbr, pbi, pcr, pci, apr, api


def _s5(p, lam_re, lam_im, log_step, b_re, b_im, c_re, c_im, d_skip):
    t = p.shape[0]
    rows = t // S5_BLK
    nsteps = max(1, int(math.ceil(math.log2(rows))))
    assert nsteps <= 16 and S5_OCT * S5_GROUP == 128
    nq, sw = S5_GROUPS // S5_OCT, 2 * S5_OCT_STATE
    tables = _s5_tables(lam_re, lam_im, log_step, b_re, b_im, c_re, c_im, nsteps)
    per_oct = lambda *shape: pl.BlockSpec((None,) + shape, lambda i: (i,) + (0,) * len(shape))
    return pl.pallas_call(
        functools.partial(_s5_kernel, rows=rows, nsteps=nsteps),
        out_shape=jax.ShapeDtypeStruct((t, S5_CH), F32),
        grid=(nq,),
        in_specs=[pl.BlockSpec((t, 128), lambda i: (0, C_SU // 128 + i)),
                  per_oct(128, S5_BLK * 128), per_oct(128, sw), per_oct(sw, 128)]
                 + [per_oct(16, sw)] * 6 + [per_oct(1, 128)],
        out_specs=pl.BlockSpec((t, 128), lambda i: (0, i)),
        scratch_shapes=[pltpu.VMEM((rows, S5_BLK * 128), F32)],
        compiler_params=_cp("arbitrary"),
        name="s5_scan",
    )(p, *tables, d_skip.reshape(nq, 1, 128))


def _ssd_kernel(z_ref, x_ref, b_ref, c_ref, sm_ref, cwx_ref, cwb_ref, cwc_ref, cbx_ref, cbb_ref, cbc_ref,
                dtb_ref, alog_ref, dsk_ref, nw_ref, o_ref, st_ref, tx_ref, tb_ref, tc_ref):
    @pl.when(pl.program_id(0) == 0)
    def _():
        st_ref[...] = jnp.zeros_like(st_ref)
        tx_ref[...] = jnp.zeros_like(tx_ref)
        tb_ref[...] = jnp.zeros_like(tb_ref)
        tc_ref[...] = jnp.zeros_like(tc_ref)

    c = SSD_CHUNK
    xin, bin_, cin = x_ref[...], b_ref[...], c_ref[...]
    xs = _silu(_causal_conv(xin, tx_ref[...], cwx_ref[...]) + cbx_ref[...])
    bs = _silu(_causal_conv(bin_, tb_ref[...], cwb_ref[...]) + cbb_ref[...])
    cs = _silu(_causal_conv(cin, tc_ref[...], cwc_ref[...]) + cbc_ref[...])
    tx_ref[...] = xin[c - 8:, :]
    tb_ref[...] = bin_[c - 8:, :]
    tc_ref[...] = cin[c - 8:, :]

    ii = lax.broadcasted_iota(jnp.int32, (c, c), 0)
    jj = lax.broadcasted_iota(jnp.int32, (c, c), 1)
    lower = ii >= jj
    dt_heads = _softplus(sm_ref[...] + dtb_ref[...])
    acum_heads = _chunk_cumsum(dt_heads * (-jnp.exp(alog_ref[...])), c)
    lane = lax.broadcasted_iota(jnp.int32, (c, 128), 1)
    lo = lane < SSD_HD
    npair = SSD_HEADS // 2
    per_group = npair // SSD_GROUPS
    pairs = range(npair)
    cols_of = [slice(pi * 128, (pi + 1) * 128) for pi in pairs]
    bgs = [bs[:, gi * SSD_STATE:(gi + 1) * SSD_STATE] for gi in range(SSD_GROUPS)]
    cgs = [cs[:, gi * SSD_STATE:(gi + 1) * SSD_STATE] for gi in range(SSD_GROUPS)]
    cbs = [_bdot_nt(cgs[gi], bgs[gi]) for gi in range(SSD_GROUPS)]
    col = lambda a, h: a[:, SSD_DT_LANE + h:SSD_DT_LANE + h + 1]
    acs = [jnp.where(lo, col(acum_heads, 2 * pi), col(acum_heads, 2 * pi + 1)) for pi in pairs]
    xds = [xs[:, cols_of[pi]] * jnp.where(lo, col(dt_heads, 2 * pi), col(dt_heads, 2 * pi + 1)) for pi in pairs]
    las = [_masked_decay(_pair_diff(col(acum_heads, 2 * pi), lane), lower) for pi in pairs]
    lbs = [_masked_decay(_pair_diff(col(acum_heads, 2 * pi + 1), lane), lower) for pi in pairs]
    sts = [st_ref[pi] for pi in pairs]
    y_off = [_bdot(cgs[pi // per_group], sts[pi]) for pi in pairs]
    y_a = [_bdot(cbs[pi // per_group] * las[pi], jnp.where(lo, xds[pi], 0.0)) for pi in pairs]
    y_b = [_bdot(cbs[pi // per_group] * lbs[pi], jnp.where(lo, 0.0, xds[pi])) for pi in pairs]
    alasts = [ac[c - 1:c, :] for ac in acs]
    st_in = [_bdot_tn(bgs[pi // per_group], xds[pi] * jnp.exp(alasts[pi] - acs[pi])) for pi in pairs]
    ys = []
    for pi in pairs:
        st_ref[pi] = sts[pi] * jnp.exp(alasts[pi]) + st_in[pi]
        y = y_a[pi] + y_b[pi] + y_off[pi] * jnp.exp(acs[pi])
        ys.append((y + dsk_ref[:, cols_of[pi]] * xs[:, cols_of[pi]]) * _silu(z_ref[:, cols_of[pi]]))
    for gi in range(SSD_GROUPS):
        members = range(gi * per_group, (gi + 1) * per_group)
        ss = sum(jnp.sum(ys[pi] * ys[pi], axis=-1, keepdims=True) for pi in members)
        scale = lax.rsqrt(ss / (SSD_INNER // SSD_GROUPS) + RMS_EPS)
        for pi in members:
            o_ref[:, cols_of[pi]] = (ys[pi] * scale * nw_ref[:, cols_of[pi]]).astype(o_ref.dtype)


def _ssd(p, conv_w, conv_b, dt_bias, a_log, d_skip, norm_w):
    t = p.shape[0]
    c = SSD_CHUNK
    nbc = SSD_GROUPS * SSD_STATE
    taps = lambda w: jnp.pad(w.T, ((0, 8 - CONV_WIDTH), (0, 0)))
    cwx, cwb, cwc = taps(conv_w[:SSD_INNER]), taps(conv_w[SSD_INNER:SSD_INNER + nbc]), taps(conv_w[SSD_INNER + nbc:])
    cbx = conv_b[:SSD_INNER].reshape(1, -1)
    cbb = conv_b[SSD_INNER:SSD_INNER + nbc].reshape(1, -1)
    cbc = conv_b[SSD_INNER + nbc:].reshape(1, -1)
    on_dt_lanes = lambda a: jnp.pad(a, (SSD_DT_LANE, 128 - SSD_DT_LANE - SSD_HEADS)).reshape(1, 128)
    full = lambda n: pl.BlockSpec((1, n), lambda i: (0, 0))
    return pl.pallas_call(
        _ssd_kernel,
        out_shape=jax.ShapeDtypeStruct((t, SSD_INNER), BF16),
        grid=(t // c,),
        in_specs=[pl.BlockSpec((c, SSD_INNER), lambda i: (i, C_MZ // SSD_INNER)),
                  pl.BlockSpec((c, SSD_INNER), lambda i: (i, C_MX // SSD_INNER)),
                  pl.BlockSpec((c, nbc), lambda i: (i, C_MB // nbc)),
                  pl.BlockSpec((c, nbc), lambda i: (i, C_MC // nbc)),
                  pl.BlockSpec((c, 128), lambda i: (i, C_SMALL // 128)),
                  pl.BlockSpec((8, SSD_INNER), lambda i: (0, 0)),
                  pl.BlockSpec((8, nbc), lambda i: (0, 0)),
                  pl.BlockSpec((8, nbc), lambda i: (0, 0)),
                  full(SSD_INNER), full(nbc), full(nbc),
                  full(128), full(128), full(SSD_INNER), full(SSD_INNER)],
        out_specs=pl.BlockSpec((c, SSD_INNER), lambda i: (i, 0)),
        scratch_shapes=[pltpu.VMEM((SSD_HEADS // 2, SSD_STATE, 128), F32),
                        pltpu.VMEM((8, SSD_INNER), F32), pltpu.VMEM((8, nbc), F32), pltpu.VMEM((8, nbc), F32)],
        compiler_params=_cp("arbitrary"),
        name="ssd",
    )(p, p, p, p, p, cwx, cwb, cwc, cbx, cbb, cbc, on_dt_lanes(dt_bias), on_dt_lanes(a_log),
      jnp.repeat(d_skip, SSD_HD).reshape(1, SSD_INNER), norm_w.reshape(1, SSD_INNER))


def kernel(x, positions, w_in, gdn_conv_w, gdn_a_log, gdn_dt_bias, gdn_norm_w, s5_lam_re, s5_lam_im, s5_log_step, s5_b_re, s5_b_im, s5_c_re, s5_c_im, s5_d, s5_glu_w, s5_glu_b, ssd_conv_w, ssd_conv_b, ssd_dt_bias, ssd_a_log, ssd_d, ssd_norm_w, w_branch_ret, w_branch_gdn, w_branch_s5, w_branch_ssd, w_out, ln1_g, ln1_b, w_up, w_down, ln2_g, ln2_b):
    bsz, t, d = x.shape
    assert bsz == 1 and d == D_MODEL
    xf = x.reshape(t, d)
    xb = xf.astype(BF16)
    cos, sin = _rope_tables(positions.reshape(t), min(t, 1024))
    w_branch = [w.astype(BF16) for w in (w_branch_ret, w_branch_gdn, w_branch_s5, w_branch_ssd)]
    w_glu, w_out_b, w_down_b = s5_glu_w.astype(BF16), w_out.astype(BF16), w_down.astype(BF16)
    for l in range(DEPTH):
        p = _proj_in(xb, w_in, l, 1024)
        y_ret = _retention(p, cos, sin, 512)
        y_gdn = _gdn(p, gdn_conv_w[l], gdn_a_log[l], gdn_dt_bias[l], gdn_norm_w[l], 512)
        v_s5 = _s5(p, s5_lam_re[l], s5_lam_im[l], s5_log_step[l], s5_b_re[l], s5_b_im[l],
                   s5_c_re[l], s5_c_im[l], s5_d[l])
        y_s5 = _glu(v_s5, w_glu, l, s5_glu_b[l], 512)
        y_ssd = _ssd(p, ssd_conv_w[l], ssd_conv_b[l], ssd_dt_bias[l], ssd_a_log[l], ssd_d[l], ssd_norm_w[l])
        merged = _merge(p, [y_ret, y_gdn, y_s5, y_ssd], w_branch, l, 1024, 512)
        xf, xb = _matmul_res_ln(merged, w_out_b, l, xf, ln1_g[l], ln1_b[l], 512, D_MODEL)
        hid = _up_proj(xb, w_up, l, 1024, 1024)
        xf, xb = _matmul_res_ln(hid, w_down_b, l, xf, ln2_g[l], ln2_b[l], 512, 2048)
    return xf.reshape(bsz, t, d)
```

```python
import functools
import math

import numpy as np
import jax
import jax.numpy as jnp
from jax import lax
from jax.experimental import pallas as pl
from jax.experimental.pallas import tpu as pltpu

F32 = jnp.float32
BF16 = jnp.bfloat16
HI = lax.Precision.HIGHEST

D_MODEL = 2048
DEPTH = 2
RET_HEADS, RET_QK, RET_V, RET_CHUNK = 4, 128, 256, 128
ROPE_BASE = 10000.0
GDN_HEADS, GDN_DIM, GDN_CHUNK = 8, 128, 64
CONV_WIDTH = 4
S5_CH, S5_GROUP, S5_STATE = 1024, 16, 64
S5_GROUPS = S5_CH // S5_GROUP
S5_BLK = 16
S5_OCT = 8
S5_OCT_STATE = S5_OCT * S5_STATE
SSD_HEADS, SSD_HD, SSD_GROUPS, SSD_STATE, SSD_CHUNK = 16, 64, 2, 128, 128
SSD_INNER = SSD_HEADS * SSD_HD
D_FF = 4 * D_MODEL
ALPHA = (2 * DEPTH) ** 0.25
LN_EPS = 1e-5
RMS_EPS = 1e-6

C_0 = 1024
C_RQ, C_RK, C_RV, C_RG = C_0, C_0 + 512, C_0 + 1024, C_0 + 2048
C_DQ, C_DK, C_DV, C_DZ = C_0 + 3072, C_0 + 4096, C_0 + 5120, C_0 + 6144
C_SU, C_MZ, C_MX, C_MB, C_MC = C_0 + 7168, C_0 + 8192, C_0 + 9216, C_0 + 10240, C_0 + 10496
C_GATE = C_0 + 10752
C_SMALL = C_0 + 18944
SSD_DT_LANE = 16
NP = C_0 + 19456

VMEM_LIMIT = 52 * 1024 * 1024


def _cp(*sem):
    return pltpu.CompilerParams(dimension_semantics=sem, vmem_limit_bytes=VMEM_LIMIT)


def _dot(a, b, prec=None):
    return lax.dot_general(a, b, (((1,), (0,)), ((), ())), precision=prec, preferred_element_type=F32)


def _dot_nt(a, b, prec=None):
    return lax.dot_general(a, b, (((1,), (1,)), ((), ())), precision=prec, preferred_element_type=F32)


def _dot_tn(a, b, prec=None):
    return lax.dot_general(a, b, (((0,), (0,)), ((), ())), precision=prec, preferred_element_type=F32)


def _bdot(a, b):
    return _dot(a.astype(BF16), b.astype(BF16))


def _bdot_nt(a, b):
    return _dot_nt(a.astype(BF16), b.astype(BF16))


def _bdot_tn(a, b):
    return _dot_tn(a.astype(BF16), b.astype(BF16))


def _silu(x):
    return x * (1.0 / (1.0 + jnp.exp(-x)))


def _sigmoid(x):
    return 1.0 / (1.0 + jnp.exp(-x))


def _softplus(x):
    return jnp.maximum(x, 0.0) + jnp.log1p(jnp.exp(-jnp.abs(x)))


def _masked_decay(diff, lower):
    return jnp.where(lower, jnp.exp(jnp.where(lower, diff, 0.0)), 0.0)


def _pair_diff(col, lane):
    hi = col.astype(BF16).astype(F32)
    r1 = col - hi
    mid = r1.astype(BF16).astype(F32)
    lo = (r1 - mid).astype(BF16).astype(F32)
    a = jnp.where(lane == 0, hi, jnp.where(lane == 1, mid, jnp.where(lane == 2, lo, jnp.where(lane < 6, 1.0, 0.0))))
    b = jnp.where(lane < 3, 1.0, jnp.where(lane == 3, -hi, jnp.where(lane == 4, -mid, jnp.where(lane == 5, -lo, 0.0))))
    return _bdot_nt(a, b)


def _chunk_cumsum(x, chunk):
    row = lax.broadcasted_iota(jnp.int32, x.shape, 0) & (chunk - 1)
    sh = 1
    while sh < chunk:
        x = x + jnp.where(row >= sh, pltpu.roll(x, sh, 0), 0.0)
        sh *= 2
    return x


def _shifted(x, tail, s, row8):
    xs = pltpu.roll(x, s, 0)
    ts = pltpu.roll(tail, s, 0)
    first = jnp.where(row8 < s, ts, xs[:8])
    return jnp.concatenate([first, xs[8:]], axis=0)


def _causal_conv(x, tail, w):
    row8 = lax.broadcasted_iota(jnp.int32, (8, x.shape[1]), 0)
    acc = x * w[3:4, :]
    for s in (1, 2, 3):
        acc = acc + _shifted(x, tail, s, row8) * w[3 - s:4 - s, :]
    return acc


def _stage_first_tile(w0_ref, wb_ref, chunk):
    for r in range(0, w0_ref.shape[0], chunk):
        wb_ref[0, r:r + chunk, :] = w0_ref[r:r + chunk, :].astype(BF16)


def _up_kernel(a_ref, w0_ref, wn_ref, o_ref, wb_ref, *, chunk):
    j, i = pl.program_id(0), pl.program_id(1)

    @pl.when((j == 0) & (i == 0))
    def _():
        _stage_first_tile(w0_ref, wb_ref, chunk)

    rows = pl.ds(pl.multiple_of(i * chunk, chunk), chunk)
    wb_ref[(j + 1) % 2, rows, :] = wn_ref[rows, :].astype(BF16)
    acc = jnp.dot(a_ref[...], wb_ref[j % 2], preferred_element_type=F32)
    o_ref[...] = jnp.square(jnp.maximum(acc, 0.0)).astype(o_ref.dtype)


def _up_proj(a, w, layer, tm, tn):
    m, k = a.shape
    n = w.shape[2]
    nj, ni = n // tn, m // tm
    chunk = k // ni
    assert chunk * ni == k and chunk % 16 == 0
    return pl.pallas_call(
        functools.partial(_up_kernel, chunk=chunk),
        out_shape=jax.ShapeDtypeStruct((m, n), BF16),
        grid=(nj, ni),
        in_specs=[pl.BlockSpec((tm, k), lambda j, i: (i, 0)),
                  pl.BlockSpec((None, k, tn), lambda j, i: (layer, 0, 0)),
                  pl.BlockSpec((None, k, tn), lambda j, i: (layer, 0, jnp.minimum(j + 1, nj - 1)))],
        out_specs=pl.BlockSpec((tm, tn), lambda j, i: (i, j)),
        scratch_shapes=[pltpu.VMEM((2, k, tn), BF16)],
        compiler_params=_cp("arbitrary", "arbitrary"),
        name="up_proj",
    )(a, w, w)


PROJ_TN = 512
PROJ_T1, PROJ_T2, PROJ_TAIL = (C_SU - C_0) // PROJ_TN, (C_GATE - C_0) // PROJ_TN, (C_SMALL - C_0) // PROJ_TN
D_IN_PROJ = 18976
SRC_DA, SRC_MDT = 7168, 10768


def _proj_kernel(a_ref, wa0_ref, wn0_ref, wa1_ref, wn1_ref, o_ref, wb_ref, *, chunk):
    j, i = pl.program_id(0), pl.program_id(1)
    body = PROJ_TN - 128

    @pl.when((j == 0) & (i == 0))
    def _():
        wb_ref[1] = jnp.zeros(wb_ref.shape[1:], BF16)

    rows = pl.ds(pl.multiple_of(i * chunk, chunk), chunk)
    lane = lax.broadcasted_iota(jnp.int32, (chunk, 128), 1)
    slot = j % 2
    for half, (wa_ref, wn_ref) in enumerate(((wa0_ref, wn0_ref), (wa1_ref, wn1_ref))):
        t = 2 * j + half
        c0 = half * PROJ_TN
        sh = jnp.where(t < PROJ_T1, 0, jnp.where(t < PROJ_T2, 16, 32))
        ra = pltpu.roll(wa_ref[rows, :], jnp.where(sh == 0, 0, PROJ_TN - sh), 1)
        rn = pltpu.roll(wn_ref[rows, :], jnp.where(sh == 0, 0, 128 - sh), 1)
        wb_ref[slot, rows, c0:c0 + body] = ra[:, :body].astype(BF16)
        wb_ref[slot, rows, c0 + body:c0 + PROJ_TN] = jnp.where(lane < 128 - sh, ra[:, body:], rn).astype(BF16)

    @pl.when(2 * j + 1 == PROJ_TAIL)
    def _():
        first = jnp.where(lane < 16, wa1_ref[rows, :128], jnp.where(lane < 32, wn1_ref[rows, :], 0.0))
        wb_ref[slot, rows, PROJ_TN:PROJ_TN + 128] = first.astype(BF16)
        wb_ref[slot, rows, PROJ_TN + 128:] = jnp.zeros((chunk, body), BF16)

    o_ref[...] = jnp.dot(a_ref[...], wb_ref[(j + 1) % 2], preferred_element_type=F32)


def _proj_in(a, w, layer, tm):
    m, k = a.shape
    assert w.shape[2] == D_IN_PROJ and NP - C_0 == (PROJ_TAIL + 1) * PROJ_TN and C_0 == 2 * PROJ_TN
    assert SRC_DA % PROJ_TN == 0 and SRC_MDT % 128 == 16 and PROJ_TAIL % 2 == 1
    ni = m // tm
    chunk = k // ni
    assert chunk * ni == k and chunk % 16 == 0
    per = PROJ_TN // 128
    npair = (NP - C_0) // (2 * PROJ_TN)

    def tile(j, half):
        return 2 * jnp.minimum(j, npair - 1) + half

    def wa_idx(half):
        return lambda j, i: (layer, jnp.where(tile(j, half) == PROJ_TAIL, SRC_DA // PROJ_TN, tile(j, half)))

    def wn_idx(half):
        return lambda j, i: (layer, jnp.where(tile(j, half) == PROJ_TAIL, SRC_MDT // 128, (tile(j, half) + 1) * per))

    w2 = w.reshape(-1, D_IN_PROJ)
    w_specs = []
    for half in range(2):
        w_specs += [pl.BlockSpec((k, PROJ_TN), wa_idx(half)), pl.BlockSpec((k, 128), wn_idx(half))]
    return pl.pallas_call(
        functools.partial(_proj_kernel, chunk=chunk),
        out_shape=jax.ShapeDtypeStruct((m, NP), F32),
        grid=(npair + 1, ni),
        in_specs=[pl.BlockSpec((tm, k), lambda j, i: (i, 0))] + w_specs,
        out_specs=pl.BlockSpec((tm, 2 * PROJ_TN), lambda j, i: (i, j)),
        scratch_shapes=[pltpu.VMEM((2, k, 2 * PROJ_TN), BF16)],
        compiler_params=_cp("arbitrary", "arbitrary"),
        name="proj_in",
    )(a, w2, w2, w2, w2)


def _mm_ln_kernel(a_ref, b_ref, res_ref, g_ref, beta_ref, of_ref, ob_ref, acc_ref, *, nk):
    k = pl.program_id(1)

    @pl.when(k == 0)
    def _():
        acc_ref[...] = jnp.zeros_like(acc_ref)

    acc_ref[...] += jnp.dot(a_ref[...], b_ref[...], preferred_element_type=F32)

    @pl.when(k == nk - 1)
    def _():
        y = ALPHA * res_ref[...] + acc_ref[...]
        mu = jnp.mean(y, axis=-1, keepdims=True)
        d = y - mu
        var = jnp.mean(d * d, axis=-1, keepdims=True)
        out = d * lax.rsqrt(var + LN_EPS) * g_ref[...] + beta_ref[...]
        of_ref[...] = out
        ob_ref[...] = out.astype(BF16)


def _matmul_res_ln(a, b, layer, res, g, beta, tm, tk):
    m, k = a.shape
    n = b.shape[2]
    nk = k // tk
    return pl.pallas_call(
        functools.partial(_mm_ln_kernel, nk=nk),
        out_shape=(jax.ShapeDtypeStruct((m, n), F32), jax.ShapeDtypeStruct((m, n), BF16)),
        grid=(m // tm, nk),
        in_specs=[pl.BlockSpec((tm, tk), lambda i, kk: (i, kk)),
                  pl.BlockSpec((None, tk, n), lambda i, kk: (layer, kk, 0)),
                  pl.BlockSpec((tm, n), lambda i, kk: (i, 0)),
                  pl.BlockSpec((1, n), lambda i, kk: (0, 0)),
                  pl.BlockSpec((1, n), lambda i, kk: (0, 0))],
        out_specs=(pl.BlockSpec((tm, n), lambda i, kk: (i, 0)),
                   pl.BlockSpec((tm, n), lambda i, kk: (i, 0))),
        scratch_shapes=[pltpu.VMEM((tm, n), F32)],
        compiler_params=_cp("arbitrary", "arbitrary"),
        name="matmul_res_ln",
    )(a, b, res, g.reshape(1, n), beta.reshape(1, n))


def _merge_kernel(y0, y1, y2, y3, w0, w1, w2, w3, g0, g1, g2, g3, o_ref):
    acc = None
    for y_ref, w_ref, g_ref in ((y0, w0, g0), (y1, w1, g1), (y2, w2, g2), (y3, w3, g3)):
        t = _sigmoid(g_ref[...]) * jnp.dot(y_ref[...], w_ref[...], preferred_element_type=F32)
        acc = t if acc is None else acc + t
    o_ref[...] = acc.astype(o_ref.dtype)


def _merge(p, ys, ws, layer, tm, tn):
    t = p.shape[0]
    kb = ys[0].shape[1]
    gate_blk = C_GATE // tn
    per_branch = D_MODEL // tn
    y_specs = [pl.BlockSpec((tm, kb), lambda i, j: (i, 0)) for _ in range(4)]
    w_specs = [pl.BlockSpec((None, kb, tn), lambda i, j: (layer, 0, j)) for _ in range(4)]
    g_specs = [pl.BlockSpec((tm, tn), functools.partial(lambda i, j, b: (i, gate_blk + b * per_branch + j), b=b))
               for b in range(4)]
    return pl.pallas_call(
        _merge_kernel,
        out_shape=jax.ShapeDtypeStruct((t, D_MODEL), BF16),
        grid=(t // tm, D_MODEL // tn),
        in_specs=y_specs + w_specs + g_specs,
        out_specs=pl.BlockSpec((tm, tn), lambda i, j: (i, j)),
        compiler_params=_cp("arbitrary", "arbitrary"),
        name="merge",
    )(*ys, *ws, p, p, p, p)


def _glu_kernel(v_ref, w_ref, b_ref, o_ref):
    v = v_ref[...]
    z = jnp.dot(v.astype(BF16), w_ref[...], preferred_element_type=F32) + b_ref[...]
    o_ref[...] = (v * _sigmoid(z)).astype(o_ref.dtype)


def _glu(v, w, layer, b, tm):
    t, n = v.shape
    return pl.pallas_call(
        _glu_kernel,
        out_shape=jax.ShapeDtypeStruct((t, n), BF16),
        grid=(t // tm,),
        in_specs=[pl.BlockSpec((tm, n), lambda i: (i, 0)),
                  pl.BlockSpec((None, n, n), lambda i: (layer, 0, 0)),
                  pl.BlockSpec((1, n), lambda i: (0, 0))],
        out_specs=pl.BlockSpec((tm, n), lambda i: (i, 0)),
        compiler_params=_cp("arbitrary"),
        name="s5_glu",
    )(v, w, b.reshape(1, n))


def _rope_kernel(pos_ref, inv_ref, cos_ref, sin_ref):
    ang = pos_ref[...].astype(F32) * inv_ref[...]
    lane = lax.broadcasted_iota(jnp.int32, ang.shape, 1)
    s = jnp.sin(ang)
    cos_ref[...] = jnp.cos(ang)
    sin_ref[...] = jnp.where(lane < RET_QK // 2, -s, s)


def _rope_tables(positions, tb):
    t = positions.shape[0]
    half = RET_QK // 2
    inv = (ROPE_BASE ** (-np.arange(half, dtype=np.float64) / half)).astype(np.float32)
    inv = jnp.asarray(np.concatenate([inv, inv])[None, :])
    return pl.pallas_call(
        _rope_kernel,
        out_shape=(jax.ShapeDtypeStruct((t, RET_QK), F32), jax.ShapeDtypeStruct((t, RET_QK), F32)),
        grid=(t // tb,),
        in_specs=[pl.BlockSpec((tb, 1), lambda i: (i, 0)),
                  pl.BlockSpec((1, RET_QK), lambda i: (0, 0))],
        out_specs=(pl.BlockSpec((tb, RET_QK), lambda i: (i, 0)),
                   pl.BlockSpec((tb, RET_QK), lambda i: (i, 0))),
        compiler_params=_cp("arbitrary"),
        name="rope_tables",
    )(positions.reshape(t, 1), inv)


def _ret_kernel(q_ref, k_ref, v_ref, g_ref, cos_ref, sin_ref, dmat_ref, qdec_ref, kdec_ref, gam_ref,
                o_ref, r_ref, *, nchunk):
    @pl.when(pl.program_id(1) == 0)
    def _():
        r_ref[...] = jnp.zeros_like(r_ref)

    c = RET_CHUNK
    dmat = dmat_ref[...]
    qdec = qdec_ref[...]
    kdec = kdec_ref[...]
    gam = gam_ref[0:1, :]
    for ci in range(nchunk):
        rows = slice(ci * c, (ci + 1) * c)
        cos = cos_ref[rows, :]
        sin = sin_ref[rows, :]
        q = q_ref[rows, :]
        k = k_ref[rows, :]
        v = v_ref[rows, :]
        qr = q * cos + pltpu.roll(q, RET_QK // 2, 1) * sin
        kr = (k * cos + pltpu.roll(k, RET_QK // 2, 1) * sin) * (RET_QK ** -0.5)
        r = r_ref[...]
        scores = _bdot_nt(qr, kr) * dmat
        y = _bdot(scores, v) + _bdot(qr * qdec, r)
        r_ref[...] = r * gam + _bdot_tn(kr * kdec, v)
        mu = jnp.mean(y, axis=-1, keepdims=True)
        d = y - mu
        var = jnp.mean(d * d, axis=-1, keepdims=True)
        yn = d * lax.rsqrt(var + LN_EPS)
        o_ref[rows, :] = (_silu(g_ref[rows, :]) * yn).astype(o_ref.dtype)


def _ret_constants():
    c = RET_CHUNK
    h = np.arange(RET_HEADS, dtype=np.float64)
    log_g = np.log(1.0 - 2.0 ** (-5.0 - h))
    idx = np.arange(c, dtype=np.float64)
    diff = idx[:, None] - idx[None, :]
    dmat = np.where(diff >= 0, np.exp(np.where(diff >= 0, diff, 0.0)[None] * log_g[:, None, None]), 0.0)
    qdec = np.exp((idx + 1.0)[None, :] * log_g[:, None])
    kdec = np.exp((c - 1.0 - idx)[None, :] * log_g[:, None])
    gam = np.exp(c * log_g)
    qdec = np.broadcast_to(qdec[:, :, None], (RET_HEADS, c, RET_QK))
    kdec = np.broadcast_to(kdec[:, :, None], (RET_HEADS, c, RET_QK))
    gam = np.broadcast_to(gam[:, None, None], (RET_HEADS, 8, RET_V))
    f = lambda a: jnp.asarray(np.ascontiguousarray(a), dtype=F32)
    return f(dmat), f(qdec), f(kdec), f(gam)


def _retention(p, cos, sin, tb):
    t = p.shape[0]
    dmat, qdec, kdec, gam = _ret_constants()
    c = RET_CHUNK
    return pl.pallas_call(
        functools.partial(_ret_kernel, nchunk=tb // c),
        out_shape=jax.ShapeDtypeStruct((t, RET_HEADS * RET_V), BF16),
        grid=(RET_HEADS, t // tb),
        in_specs=[pl.BlockSpec((tb, RET_QK), lambda h, i: (i, C_RQ // RET_QK + h)),
                  pl.BlockSpec((tb, RET_QK), lambda h, i: (i, C_RK // RET_QK + h)),
                  pl.BlockSpec((tb, RET_V), lambda h, i: (i, C_RV // RET_V + h)),
                  pl.BlockSpec((tb, RET_V), lambda h, i: (i, C_RG // RET_V + h)),
                  pl.BlockSpec((tb, RET_QK), lambda h, i: (i, 0)),
                  pl.BlockSpec((tb, RET_QK), lambda h, i: (i, 0)),
                  pl.BlockSpec((None, c, c), lambda h, i: (h, 0, 0)),
                  pl.BlockSpec((None, c, RET_QK), lambda h, i: (h, 0, 0)),
                  pl.BlockSpec((None, c, RET_QK), lambda h, i: (h, 0, 0)),
                  pl.BlockSpec((None, 8, RET_V), lambda h, i: (h, 0, 0))],
        out_specs=pl.BlockSpec((tb, RET_V), lambda h, i: (i, h)),
        scratch_shapes=[pltpu.VMEM((RET_QK, RET_V), F32)],
        compiler_params=_cp("arbitrary", "arbitrary"),
        name="retention",
    )(p, p, p, p, cos, sin, dmat, qdec, kdec, gam)


GDN_SUB = 16


def _gdn_kernel(q_ref, k_ref, v_ref, z_ref, sm_ref, cw_ref, hp_ref, nw_ref, o_ref,
                s_ref, tq_ref, tk_ref, tv_ref, *, nchunk):
    h = pl.program_id(0)

    @pl.when(pl.program_id(1) == 0)
    def _():
        s_ref[...] = jnp.zeros_like(s_ref)
        tq_ref[...] = jnp.zeros_like(tq_ref)
        tk_ref[...] = jnp.zeros_like(tk_ref)
        tv_ref[...] = jnp.zeros_like(tv_ref)

    c = GDN_CHUNK
    tb = nchunk * c
    xq, xk, xv = q_ref[...], k_ref[...], v_ref[...]
    q = _silu(_causal_conv(xq, tq_ref[...], cw_ref[0]))
    k = _silu(_causal_conv(xk, tk_ref[...], cw_ref[1]))
    v = _silu(_causal_conv(xv, tv_ref[...], cw_ref[2]))
    tq_ref[...] = xq[tb - 8:, :]
    tk_ref[...] = xk[tb - 8:, :]
    tv_ref[...] = xv[tb - 8:, :]
    q = q * lax.rsqrt(jnp.sum(q * q, axis=-1, keepdims=True) + RMS_EPS) * (GDN_DIM ** -0.5)
    k = k * lax.rsqrt(jnp.sum(k * k, axis=-1, keepdims=True) + RMS_EPS)

    sm = sm_ref[...]
    lane_b = lax.broadcasted_iota(jnp.int32, sm.shape, 1)
    g_heads = -jnp.exp(hp_ref[0:1, :]) * _softplus(sm + hp_ref[1:2, :])
    cum_heads = _chunk_cumsum(g_heads, c)
    gcum = jnp.sum(jnp.where(lane_b == h, cum_heads, 0.0), axis=-1, keepdims=True)
    beta = _sigmoid(jnp.sum(jnp.where(lane_b == h + GDN_HEADS, sm, 0.0), axis=-1, keepdims=True))
    eg = jnp.exp(gcum)

    ii = lax.broadcasted_iota(jnp.int32, (c, c), 0)
    jj = lax.broadcasted_iota(jnp.int32, (c, c), 1)
    lower = ii >= jj
    strict = ii > jj
    same_sub = jnp.right_shift(ii, 4) == jnp.right_shift(jj, 4)
    eye = (ii == jj).astype(F32)
    lane = lax.broadcasted_iota(jnp.int32, (c, GDN_DIM), 1)
    nw = nw_ref[...]

    chunks = range(nchunk)
    rows_of = [slice(ci * c, (ci + 1) * c) for ci in chunks]
    qs, ks, vs = [q[r] for r in rows_of], [k[r] for r in rows_of], [v[r] for r in rows_of]
    bs, gs, egs = [beta[r] for r in rows_of], [gcum[r] for r in rows_of], [eg[r] for r in rows_of]
    gammas = [_masked_decay(_pair_diff(gs[ci], lane), lower) for ci in chunks]
    ms = [bs[ci] * _bdot_nt(ks[ci], ks[ci]) * gammas[ci] for ci in chunks]
    qks = [_bdot_nt(qs[ci], ks[ci]) * gammas[ci] for ci in chunks]
    pws = [jnp.where(strict & same_sub, -m, 0.0) for m in ms]
    loffs = [jnp.where(strict & jnp.logical_not(same_sub), m, 0.0) for m in ms]
    dinvs = [eye + pw for pw in pws]
    for _ in range(3):
        pws = [_bdot(pw, pw) for pw in pws]
        dinvs = [dinvs[ci] + _bdot(dinvs[ci], pws[ci]) for ci in chunks]
    es = [-_bdot(dinvs[ci], loffs[ci]) for ci in chunks]
    ys = [_bdot(dinvs[ci], jnp.concatenate([vs[ci] * bs[ci], ks[ci] * (bs[ci] * egs[ci])], axis=1)) for ci in chunks]
    e2s = [_bdot(e, e) for e in es]
    ys = [ys[ci] + _bdot(es[ci], ys[ci]) for ci in chunks]
    ys = [ys[ci] + _bdot(e2s[ci], ys[ci]) for ci in chunks]
    glasts = [g[c - 1:c, :] for g in gs]
    kty = [_bdot_tn(ks[ci] * jnp.exp(glasts[ci] - gs[ci]), ys[ci]) for ci in chunks]
    qky = [_bdot(qks[ci], ys[ci]) for ci in chunks]
    s = s_ref[...]
    for ci in chunks:
        sb = s.astype(BF16)
        s_next = s * jnp.exp(glasts[ci]) + kty[ci][:, :GDN_DIM] - _dot(kty[ci][:, GDN_DIM:].astype(BF16), sb)
        o = _dot((qs[ci] * egs[ci] - qky[ci][:, GDN_DIM:]).astype(BF16), sb) + qky[ci][:, :GDN_DIM]
        s = s_next
        o = o * lax.rsqrt(jnp.mean(o * o, axis=-1, keepdims=True) + RMS_EPS) * nw
        o_ref[rows_of[ci], :] = (o * _silu(z_ref[rows_of[ci], :])).astype(o_ref.dtype)
    s_ref[...] = s


def _gdn(p, conv_w, a_log, dt_bias, norm_w, tb):
    t = p.shape[0]
    hd = GDN_DIM
    assert GDN_CHUNK == 4 * GDN_SUB
    cw = conv_w.reshape(3, GDN_HEADS, hd, CONV_WIDTH).transpose(0, 1, 3, 2)
    cw = jnp.pad(cw, ((0, 0), (0, 0), (0, 8 - CONV_WIDTH), (0, 0)))
    hp = jnp.pad(jnp.stack([a_log, dt_bias], axis=0), ((0, 6), (0, 128 - GDN_HEADS)))
    blk = lambda off: (lambda h, i: (i, off // hd + h))
    return pl.pallas_call(
        functools.partial(_gdn_kernel, nchunk=tb // GDN_CHUNK),
        out_shape=jax.ShapeDtypeStruct((t, GDN_HEADS * hd), BF16),
        grid=(GDN_HEADS, t // tb),
        in_specs=[pl.BlockSpec((tb, hd), blk(C_DQ)),
                  pl.BlockSpec((tb, hd), blk(C_DK)),
                  pl.BlockSpec((tb, hd), blk(C_DV)),
                  pl.BlockSpec((tb, hd), blk(C_DZ)),
                  pl.BlockSpec((tb, 128), lambda h, i: (i, C_SMALL // 128)),
                  pl.BlockSpec((3, None, 8, hd), lambda h, i: (0, h, 0, 0)),
                  pl.BlockSpec((8, 128), lambda h, i: (0, 0)),
                  pl.BlockSpec((1, hd), lambda h, i: (0, 0))],
        out_specs=pl.BlockSpec((tb, hd), lambda h, i: (i, h)),
        scratch_shapes=[pltpu.VMEM((hd, hd), F32), pltpu.VMEM((8, hd), F32),
                        pltpu.VMEM((8, hd), F32), pltpu.VMEM((8, hd), F32)],
        compiler_params=_cp("arbitrary", "arbitrary"),
        name="gated_deltanet",
    )(p, p, p, p, p, cw, hp, norm_w.reshape(1, hd))


def _gelu_tanh(x):
    return 0.5 * x * (1.0 + jnp.tanh(math.sqrt(2.0 / math.pi) * (x + 0.044715 * (x * x * x))))


def _crot(x, p_re, p_im):
    return p_re * x + p_im * pltpu.roll(x, S5_OCT_STATE, 1)


def _s5_kernel(u_ref, kc_ref, bb_ref, cb_ref, pbr_ref, pbi_ref, pcr_ref, pci_ref, apr_ref, api_ref, dsk_ref,
               o_ref, y_ref, *, rows, nsteps):
    blk = S5_BLK
    us = [u_ref[pl.ds(s, rows, stride=blk), :] for s in range(blk)]
    ubs = [u.astype(BF16) for u in us]
    for s in range(blk):
        w = _dot(ubs[s], kc_ref[:, :(blk - s) * 128])
        if s == 0:
            y_ref[...] = w
        else:
            y_ref[:, s * 128:] += w
    x = None
    for s in range(blk):
        term = _crot(_dot(ubs[s], bb_ref[...]), pbr_ref[s:s + 1, :], pbi_ref[s:s + 1, :])
        x = term if x is None else x + term
    row = lax.broadcasted_iota(jnp.int32, x.shape, 0)
    for kstep in range(nsteps):
        sh = 1 << kstep
        xs = jnp.where(row >= sh, pltpu.roll(x, sh, 0), 0.0)
        x = x + _crot(xs, apr_ref[kstep:kstep + 1, :], api_ref[kstep:kstep + 1, :])
    xprev = jnp.where(row >= 1, pltpu.roll(x, 1, 0), 0.0)
    for t in range(blk):
        carried = _crot(xprev, pcr_ref[t:t + 1, :], pci_ref[t:t + 1, :]).astype(BF16)
        y = y_ref[:, t * 128:(t + 1) * 128] + _dot(carried, cb_ref[...])
        o_ref[pl.ds(t, rows, stride=blk), :] = _gelu_tanh(y + dsk_ref[...] * us[t])


def _s5_tables(lam_re, lam_im, log_step, b_re, b_im, c_re, c_im, nsteps):
    g, n = lam_re.shape
    blk, qg, cg = S5_BLK, S5_OCT, S5_GROUP
    nq = g // qg
    step = jnp.exp(log_step)[:, None]
    zr, zi = lam_re * step, lam_im * step

    def powers(m):
        m = jnp.asarray(m, F32)[:, None, None]
        mag = jnp.exp(zr[None] * m)
        return mag * jnp.cos(zi[None] * m), mag * jnp.sin(zi[None] * m)

    ab_re, ab_im = jnp.exp(zr) * jnp.cos(zi), jnp.exp(zr) * jnp.sin(zi)
    den = lam_re * lam_re + lam_im * lam_im
    f_re = ((ab_re - 1.0) * lam_re + ab_im * lam_im) / den
    f_im = (ab_im * lam_re - (ab_re - 1.0) * lam_im) / den
    bb_re = f_re[..., None] * b_re - f_im[..., None] * b_im
    bb_im = f_re[..., None] * b_im + f_im[..., None] * b_re
    pr, pi = powers(np.arange(blk + 1))
    cpr = c_re[None] * pr[:, :, None, :] - c_im[None] * pi[:, :, None, :]
    cpi = c_re[None] * pi[:, :, None, :] + c_im[None] * pr[:, :, None, :]
    bt_re, bt_im = bb_re.transpose(0, 2, 1)[None, :, None], bb_im.transpose(0, 2, 1)[None, :, None]
    kern = jnp.sum(cpr[:blk, :, :, None, :] * bt_re - cpi[:blk, :, :, None, :] * bt_im, axis=-1)
    def same_group(a, b, reps=1):
        m = np.kron(np.eye(qg, dtype=np.float32), np.ones((a, b), np.float32))
        return jnp.asarray(np.tile(m, (1, reps)))

    kc = kern.reshape(blk, nq, qg, cg, cg).transpose(1, 2, 4, 0, 3)[:, :, :, :, None, :]
    kc = jnp.broadcast_to(kc, (nq, qg, cg, blk, qg, cg)).reshape(nq, qg * cg, blk * qg * cg) * same_group(cg, cg, blk)

    def spread_b(b):
        b = b.reshape(nq, qg, n, cg).transpose(0, 1, 3, 2)[:, :, :, None, :]
        return jnp.broadcast_to(b, (nq, qg, cg, qg, n)).reshape(nq, qg * cg, qg * n) * same_group(cg, n)

    def spread_c(c):
        c = c.reshape(nq, qg, cg, n).transpose(0, 1, 3, 2)[:, :, :, None, :]
        return jnp.broadcast_to(c, (nq, qg, n, qg, cg)).reshape(nq, qg * n, qg * cg) * same_group(n, cg)

    bbase = jnp.concatenate([spread_b(bb_re), spread_b(bb_im)], axis=-1)
    cbase = jnp.concatenate([spread_c(c_re), spread_c(-c_im)], axis=1)

    def rows_of(p_re, p_im):
        k = p_re.shape[0]
        p_re, p_im = p_re.reshape(k, nq, qg * n), p_im.reshape(k, nq, qg * n)
        pad = ((0, 0), (0, 16 - k), (0, 0))
        return (jnp.pad(jnp.concatenate([p_re, p_re], axis=-1).transpose(1, 0, 2), pad),
                jnp.pad(jnp.concatenate([-p_im, p_im], axis=-1).transpose(1, 0, 2), pad))

    pbr, pbi = rows_of(pr[:blk][::-1], pi[:blk][::-1])
    pcr, pci = rows_of(pr[1:], pi[1:])
    apr, api = rows_of(*powers(float(blk) * (2.0 ** np.arange(nsteps))))
    return kc.astype(BF16), bbase.astype(BF16), cbase.astype(BF16), pbr, pbi, pcr, pci, apr, api


def _s5(p, lam_re, lam_im, log_step, b_re, b_im, c_re, c_im, d_skip):
    t = p.shape[0]
    rows = t // S5_BLK
    nsteps = max(1, int(math.ceil(math.log2(rows))))
    assert nsteps <= 16 and S5_OCT * S5_GROUP == 128
    nq, sw = S5_GROUPS // S5_OCT, 2 * S5_OCT_STATE
    tables = _s5_tables(lam_re, lam_im, log_step, b_re, b_im, c_re, c_im, nsteps)
    per_oct = lambda *shape: pl.BlockSpec((None,) + shape, lambda i: (i,) + (0,) * len(shape))
    return pl.pallas_call(
        functools.partial(_s5_kernel, rows=rows, nsteps=nsteps),
        out_shape=jax.ShapeDtypeStruct((t, S5_CH), F32),
        grid=(nq,),
        in_specs=[pl.BlockSpec((t, 128), lambda i: (0, C_SU // 128 + i)),
                  per_oct(128, S5_BLK * 128), per_oct(128, sw), per_oct(sw, 128)]
                 + [per_oct(16, sw)] * 6 + [per_oct(1, 128)],
        out_specs=pl.BlockSpec((t, 128), lambda i: (0, i)),
        scratch_shapes=[pltpu.VMEM((rows, S5_BLK * 128), F32)],
        compiler_params=_cp("arbitrary"),
        name="s5_scan",
    )(p, *tables, d_skip.reshape(nq, 1, 128))


def _ssd_kernel(z_ref, x_ref, b_ref, c_ref, sm_ref, cwx_ref, cwb_ref, cwc_ref, cbx_ref, cbb_ref, cbc_ref,
                dtb_ref, alog_ref, dsk_ref, nw_ref, o_ref, st_ref, tx_ref, tb_ref, tc_ref):
    @pl.when(pl.program_id(0) == 0)
    def _():
        st_ref[...] = jnp.zeros_like(st_ref)
        tx_ref[...] = jnp.zeros_like(tx_ref)
        tb_ref[...] = jnp.zeros_like(tb_ref)
        tc_ref[...] = jnp.zeros_like(tc_ref)

    c = SSD_CHUNK
    xin, bin_, cin = x_ref[...], b_ref[...], c_ref[...]
    xs = _silu(_causal_conv(xin, tx_ref[...], cwx_ref[...]) + cbx_ref[...])
    bs = _silu(_causal_conv(bin_, tb_ref[...], cwb_ref[...]) + cbb_ref[...])
    cs = _silu(_causal_conv(cin, tc_ref[...], cwc_ref[...]) + cbc_ref[...])
    tx_ref[...] = xin[c - 8:, :]
    tb_ref[...] = bin_[c - 8:, :]
    tc_ref[...] = cin[c - 8:, :]

    ii = lax.broadcasted_iota(jnp.int32, (c, c), 0)
    jj = lax.broadcasted_iota(jnp.int32, (c, c), 1)
    lower = ii >= jj
    dt_heads = _softplus(sm_ref[...] + dtb_ref[...])
    acum_heads = _chunk_cumsum(dt_heads * (-jnp.exp(alog_ref[...])), c)
    lane = lax.broadcasted_iota(jnp.int32, (c, 128), 1)
    lo = lane < SSD_HD
    npair = SSD_HEADS // 2
    per_group = npair // SSD_GROUPS
    pairs = range(npair)
    cols_of = [slice(pi * 128, (pi + 1) * 128) for pi in pairs]
    bgs = [bs[:, gi * SSD_STATE:(gi + 1) * SSD_STATE] for gi in range(SSD_GROUPS)]
    cgs = [cs[:, gi * SSD_STATE:(gi + 1) * SSD_STATE] for gi in range(SSD_GROUPS)]
    cbs = [_bdot_nt(cgs[gi], bgs[gi]) for gi in range(SSD_GROUPS)]
    col = lambda a, h: a[:, SSD_DT_LANE + h:SSD_DT_LANE + h + 1]
    acs = [jnp.where(lo, col(acum_heads, 2 * pi), col(acum_heads, 2 * pi + 1)) for pi in pairs]
    xds = [xs[:, cols_of[pi]] * jnp.where(lo, col(dt_heads, 2 * pi), col(dt_heads, 2 * pi + 1)) for pi in pairs]
    las = [_masked_decay(_pair_diff(col(acum_heads, 2 * pi), lane), lower) for pi in pairs]
    lbs = [_masked_decay(_pair_diff(col(acum_heads, 2 * pi + 1), lane), lower) for pi in pairs]
    sts = [st_ref[pi] for pi in pairs]
    y_off = [_bdot(cgs[pi // per_group], sts[pi]) for pi in pairs]
    y_a = [_bdot(cbs[pi // per_group] * las[pi], jnp.where(lo, xds[pi], 0.0)) for pi in pairs]
    y_b = [_bdot(cbs[pi // per_group] * lbs[pi], jnp.where(lo, 0.0, xds[pi])) for pi in pairs]
    alasts = [ac[c - 1:c, :] for ac in acs]
    st_in = [_bdot_tn(bgs[pi // per_group], xds[pi] * jnp.exp(alasts[pi] - acs[pi])) for pi in pairs]
    ys = []
    for pi in pairs:
        st_ref[pi] = sts[pi] * jnp.exp(alasts[pi]) + st_in[pi]
        y = y_a[pi] + y_b[pi] + y_off[pi] * jnp.exp(acs[pi])
        ys.append((y + dsk_ref[:, cols_of[pi]] * xs[:, cols_of[pi]]) * _silu(z_ref[:, cols_of[pi]]))
    for gi in range(SSD_GROUPS):
        members = range(gi * per_group, (gi + 1) * per_group)
        ss = sum(jnp.sum(ys[pi] * ys[pi], axis=-1, keepdims=True) for pi in members)
        scale = lax.rsqrt(ss / (SSD_INNER // SSD_GROUPS) + RMS_EPS)
        for pi in members:
            o_ref[:, cols_of[pi]] = (ys[pi] * scale * nw_ref[:, cols_of[pi]]).astype(o_ref.dtype)


def _ssd(p, conv_w, conv_b, dt_bias, a_log, d_skip, norm_w):
    t = p.shape[0]
    c = SSD_CHUNK
    nbc = SSD_GROUPS * SSD_STATE
    taps = lambda w: jnp.pad(w.T, ((0, 8 - CONV_WIDTH), (0, 0)))
    cwx, cwb, cwc = taps(conv_w[:SSD_INNER]), taps(conv_w[SSD_INNER:SSD_INNER + nbc]), taps(conv_w[SSD_INNER + nbc:])
    cbx = conv_b[:SSD_INNER].reshape(1, -1)
    cbb = conv_b[SSD_INNER:SSD_INNER + nbc].reshape(1, -1)
    cbc = conv_b[SSD_INNER + nbc:].reshape(1, -1)
    on_dt_lanes = lambda a: jnp.pad(a, (SSD_DT_LANE, 128 - SSD_DT_LANE - SSD_HEADS)).reshape(1, 128)
    full = lambda n: pl.BlockSpec((1, n), lambda i: (0, 0))
    return pl.pallas_call(
        _ssd_kernel,
        out_shape=jax.ShapeDtypeStruct((t, SSD_INNER), BF16),
        grid=(t // c,),
        in_specs=[pl.BlockSpec((c, SSD_INNER), lambda i: (i, C_MZ // SSD_INNER)),
                  pl.BlockSpec((c, SSD_INNER), lambda i: (i, C_MX // SSD_INNER)),
                  pl.BlockSpec((c, nbc), lambda i: (i, C_MB // nbc)),
                  pl.BlockSpec((c, nbc), lambda i: (i, C_MC // nbc)),
                  pl.BlockSpec((c, 128), lambda i: (i, C_SMALL // 128)),
                  pl.BlockSpec((8, SSD_INNER), lambda i: (0, 0)),
                  pl.BlockSpec((8, nbc), lambda i: (0, 0)),
                  pl.BlockSpec((8, nbc), lambda i: (0, 0)),
                  full(SSD_INNER), full(nbc), full(nbc),
                  full(128), full(128), full(SSD_INNER), full(SSD_INNER)],
        out_specs=pl.BlockSpec((c, SSD_INNER), lambda i: (i, 0)),
        scratch_shapes=[pltpu.VMEM((SSD_HEADS // 2, SSD_STATE, 128), F32),
                        pltpu.VMEM((8, SSD_INNER), F32), pltpu.VMEM((8, nbc), F32), pltpu.VMEM((8, nbc), F32)],
        compiler_params=_cp("arbitrary"),
        name="ssd",
    )(p, p, p, p, p, cwx, cwb, cwc, cbx, cbb, cbc, on_dt_lanes(dt_bias), on_dt_lanes(a_log),
      jnp.repeat(d_skip, SSD_HD).reshape(1, SSD_INNER), norm_w.reshape(1, SSD_INNER))


def kernel(x, positions, w_in, gdn_conv_w, gdn_a_log, gdn_dt_bias, gdn_norm_w, s5_lam_re, s5_lam_im, s5_log_step, s5_b_re, s5_b_im, s5_c_re, s5_c_im, s5_d, s5_glu_w, s5_glu_b, ssd_conv_w, ssd_conv_b, ssd_dt_bias, ssd_a_log, ssd_d, ssd_norm_w, w_branch_ret, w_branch_gdn, w_branch_s5, w_branch_ssd, w_out, ln1_g, ln1_b, w_up, w_down, ln2_g, ln2_b):
    bsz, t, d = x.shape
    assert bsz == 1 and d == D_MODEL
    xf = x.reshape(t, d)
    xb = xf.astype(BF16)
    cos, sin = _rope_tables(positions.reshape(t), min(t, 1024))
    w_branch = [w.astype(BF16) for w in (w_branch_ret, w_branch_gdn, w_branch_s5, w_branch_ssd)]
    w_glu, w_out_b, w_down_b = s5_glu_w.astype(BF16), w_out.astype(BF16), w_down.astype(BF16)
    for l in range(DEPTH):
        p = _proj_in(xb, w_in, l, 1024)
        y_ret = _retention(p, cos, sin, 512)
        y_gdn = _gdn(p, gdn_conv_w[l], gdn_a_log[l], gdn_dt_bias[l], gdn_norm_w[l], 512)
        v_s5 = _s5(p, s5_lam_re[l], s5_lam_im[l], s5_log_step[l], s5_b_re[l], s5_b_im[l],
                   s5_c_re[l], s5_c_im[l], s5_d[l])
        y_s5 = _glu(v_s5, w_glu, l, s5_glu_b[l], 512)
        y_ssd = _ssd(p, ssd_conv_w[l], ssd_conv_b[l], ssd_dt_bias[l], ssd_a_log[l], ssd_d[l], ssd_norm_w[l])
        merged = _merge(p, [y_ret, y_gdn, y_s5, y_ssd], w_branch, l, 1024, 512)
        xf, xb = _matmul_res_ln(merged, w_out_b, l, xf, ln1_g[l], ln1_b[l], 512, D_MODEL)
        hid = _up_proj(xb, w_up, l, 1024, 1024)
        xf, xb = _matmul_res_ln(hid, w_down_b, l, xf, ln2_g[l], ln2_b[l], 512, 2048)
    return xf.reshape(bsz, t, d)
```
